```python
import jax, jax.numpy as jnp
from jax import lax
import numpy as np

D_MODEL = 1024
BATCH = 32
SEQ = 2048
DEPTH = 4

A_HEADS = 4
A_DK = 128
A_DV = 128
CONV_K = 4
B_HEADS = 4
B_DK = 128
B_DV = 128
CHUNK = 64
C_HEADS = 16
C_KV_HEADS = 4
C_GROUP = C_HEADS // C_KV_HEADS
C_HD = 64
WINDOW = 128
ROPE_THETA = 10000.0
D_FF = 4 * D_MODEL
PLE_DIM = 256
EPS = 1e-6
N_EVEN = (DEPTH + 1) // 2
N_ODD = DEPTH // 2

A_QK = A_HEADS * A_DK
A_V = A_HEADS * A_DV
B_QK = B_HEADS * B_DK
B_V = B_HEADS * B_DV
A_CONV_CH = 2 * A_QK + A_V
A_COLS = 2 * A_QK + A_V + 2 * A_HEADS + A_V
B_COLS = 2 * B_QK + B_V + 2 * B_HEADS + B_V
IN_COLS = A_COLS + B_COLS
MIX_AB = A_V + B_V
C_Q = C_HEADS * C_HD
C_KV = C_KV_HEADS * C_HD
C_COLS = C_Q + 2 * C_KV

kernel_name = "hybrid_deltanet_mlstm_swa_trunk"

F32 = jnp.float32


def rmsnorm(x, g):
    xf = x.astype(F32)
    y = xf * lax.rsqrt(jnp.mean(xf * xf, -1, keepdims=True) + EPS)
    return (y * g.astype(F32)).astype(x.dtype)


def l2norm(x):
    return x * lax.rsqrt(jnp.sum(x * x, -1, keepdims=True) + EPS)


def causal_conv(x, w):
    k_taps = w.shape[0]
    s = x.shape[1]
    xp = jnp.pad(x, ((0, 0), (k_taps - 1, 0), (0, 0)))
    return sum(xp[:, k:k + s] * w[k] for k in range(k_taps))


def to_chunks(x):
    b, s, h = x.shape[:3]
    x = x.reshape((b, s // CHUNK, CHUNK, h) + x.shape[3:])
    return jnp.moveaxis(x, (1, 3), (0, 2))


def from_chunks(x):
    x = jnp.moveaxis(x, (0, 2), (1, 3))
    b, n, c, h, d = x.shape
    return x.reshape(b, n * c, h, d)


def gated_delta_rule(q, k, v, g, beta):
    b, s, h, dk = q.shape
    dv = v.shape[-1]
    qc, kc, vc = to_chunks(q), to_chunks(k), to_chunks(v)
    gcum = jnp.cumsum(to_chunks(g), -1)
    bc = to_chunks(beta)
    causal = jnp.tril(jnp.ones((CHUNK, CHUNK), bool))
    strict = jnp.tril(jnp.ones((CHUNK, CHUNK), bool), -1)
    diff = gcum[..., :, None] - gcum[..., None, :]
    decay = jnp.where(causal, jnp.exp(jnp.where(causal, diff, 0.0)), 0.0)
    kb = kc * bc[..., None]
    vb = vc * bc[..., None]
    lmat = jnp.where(strict, jnp.einsum('nbhcd,nbhed->nbhce', kb, kc) * decay, 0.0)
    eye = jnp.eye(CHUNK, dtype=lmat.dtype)
    rhs = jnp.concatenate([vb, kb * jnp.exp(gcum)[..., None]], -1)
    uw = lax.linalg.triangular_solve(lmat + eye, rhs, left_side=True, lower=True,
                                     unit_diagonal=True)
    u, w = uw[..., :dv], uw[..., dv:]
    attn_intra = jnp.einsum('nbhcd,nbhed->nbhce', qc, kc) * decay
    q_dec = qc * jnp.exp(gcum)[..., None]
    g_last = gcum[..., -1]
    k_dec = kc * jnp.exp(g_last[..., None] - gcum)[..., None]

    def step(state, xs):
        u_, w_, a_, qd, kd, gl = xs
        v_new = u_ - jnp.einsum('bhcd,bhde->bhce', w_, state)
        o = (jnp.einsum('bhcd,bhde->bhce', qd, state)
             + jnp.einsum('bhcs,bhse->bhce', a_, v_new))
        state = (state * jnp.exp(gl)[..., None, None]
                 + jnp.einsum('bhcd,bhce->bhde', kd, v_new))
        return state, o

    s0 = jnp.zeros((b, h, dk, dv), F32)
    _, o = lax.scan(step, s0, (u, w, attn_intra, q_dec, k_dec, g_last))
    return from_chunks(o)


def mlstm_chunkwise(q, k, v, i_log, f_log):
    b, s, h, dk = q.shape
    dv = v.shape[-1]
    qc, kc, vc = to_chunks(q), to_chunks(k), to_chunks(v)
    ic = to_chunks(i_log)
    bcum = jnp.cumsum(to_chunks(f_log), -1)
    causal = jnp.tril(jnp.ones((CHUNK, CHUNK), bool))
    dmat = bcum[..., :, None] - bcum[..., None, :] + ic[..., None, :]
    dmat = jnp.where(causal, dmat, -jnp.inf)
    m_intra = jnp.max(dmat, -1)
    b_last = bcum[..., -1]
    w_state = b_last[..., None] - bcum + ic
    qk = jnp.einsum('nbhtd,nbhsd->nbhts', qc, kc)

    def step(carry, xs):
        c_st, n_st, m = carry
        q_, k_, v_, qk_, d_, mi, bc_, ws, bl = xs
        a = bc_ + m[..., None]
        mt = jnp.maximum(a, mi)
        inter = jnp.exp(a - mt)
        pmat = qk_ * jnp.exp(d_ - mt[..., None])
        num = (inter[..., None] * jnp.einsum('bhcd,bhde->bhce', q_, c_st)
               + jnp.einsum('bhts,bhse->bhte', pmat, v_))
        den = inter * jnp.einsum('bhcd,bhd->bhc', q_, n_st) + jnp.sum(pmat, -1)
        hout = num / jnp.maximum(jnp.abs(den), jnp.exp(-mt))[..., None]
        m_new = jnp.maximum(bl + m, jnp.max(ws, -1))
        sc = jnp.exp(bl + m - m_new)
        kw = k_ * jnp.exp(ws - m_new[..., None])[..., None]
        c_st = sc[..., None, None] * c_st + jnp.einsum('bhcd,bhce->bhde', kw, v_)
        n_st = sc[..., None] * n_st + jnp.sum(kw, -2)
        return (c_st, n_st, m_new), hout

    carry0 = (jnp.zeros((b, h, dk, dv), F32), jnp.zeros((b, h, dk), F32), jnp.zeros((b, h), F32))
    _, o = lax.scan(step, carry0, (qc, kc, vc, qk, dmat, m_intra, bcum, w_state, b_last))
    return from_chunks(o)


def mixer_ab(hn, w_in, conv_w, a_log, dt_bias, norm_a, i_bias, f_bias, norm_b, w_out):
    b, s, _ = hn.shape
    proj = hn @ w_in
    pa, pb = proj[..., :A_COLS], proj[..., A_COLS:]
    qkv = jax.nn.silu(causal_conv(pa[..., :A_CONV_CH], conv_w)).astype(F32)
    qa = l2norm(qkv[..., :A_QK].reshape(b, s, A_HEADS, A_DK)) * (A_DK ** -0.5)
    ka = l2norm(qkv[..., A_QK:2 * A_QK].reshape(b, s, A_HEADS, A_DK))
    va = qkv[..., 2 * A_QK:].reshape(b, s, A_HEADS, A_DV)
    off = A_CONV_CH
    beta = jax.nn.sigmoid(pa[..., off:off + A_HEADS].astype(F32))
    alpha_pre = pa[..., off + A_HEADS:off + 2 * A_HEADS].astype(F32) + dt_bias.astype(F32)
    g = -jnp.exp(a_log.astype(F32)) * jax.nn.softplus(alpha_pre)
    z = pa[..., off + 2 * A_HEADS:].astype(F32).reshape(b, s, A_HEADS, A_DV)
    oa = gated_delta_rule(qa, ka, va, g, beta)
    oa = rmsnorm(oa, norm_a) * jax.nn.silu(z)
    pbf = pb.astype(F32)
    qb = pbf[..., :B_QK].reshape(b, s, B_HEADS, B_DK) * (B_DK ** -0.5)
    kb = pbf[..., B_QK:2 * B_QK].reshape(b, s, B_HEADS, B_DK)
    vb = pbf[..., 2 * B_QK:2 * B_QK + B_V].reshape(b, s, B_HEADS, B_DV)
    off = 2 * B_QK + B_V
    i_log = pbf[..., off:off + B_HEADS] + i_bias.astype(F32)
    f_log = jax.nn.log_sigmoid(pbf[..., off + B_HEADS:off + 2 * B_HEADS] + f_bias.astype(F32))
    o_gate = jax.nn.sigmoid(pbf[..., off + 2 * B_HEADS:]).reshape(b, s, B_HEADS, B_DV)
    ob = mlstm_chunkwise(qb, kb, vb, i_log, f_log)
    ob = rmsnorm(ob, norm_b) * o_gate
    mixed = jnp.concatenate([oa.reshape(b, s, A_V), ob.reshape(b, s, B_V)], -1)
    return mixed.astype(hn.dtype) @ w_out


def rope(x, pos):
    half = x.shape[-1] // 2
    inv = ROPE_THETA ** (-jnp.arange(half, dtype=F32) / half)
    ang = pos.astype(F32)[..., None] * inv
    cos = jnp.cos(ang)[:, :, None, :]
    sin = jnp.sin(ang)[:, :, None, :]
    x1, x2 = x[..., :half], x[..., half:]
    return jnp.concatenate([x1 * cos - x2 * sin, x2 * cos + x1 * sin], -1)


def mixer_c(hn, pos, w_qkv, b_qkv, sinks, w_o, b_o):
    b, s, _ = hn.shape
    nb = s // WINDOW
    proj = (hn @ w_qkv + b_qkv).astype(F32)
    q = rope(proj[..., :C_Q].reshape(b, s, C_HEADS, C_HD), pos)
    q = q.reshape(b, nb, WINDOW, C_KV_HEADS, C_GROUP, C_HD)
    q = jnp.moveaxis(q, 1, 0)
    k = rope(proj[..., C_Q:C_Q + C_KV].reshape(b, s, C_KV_HEADS, C_HD), pos)
    v = proj[..., C_Q + C_KV:].reshape(b, s, C_KV_HEADS, C_HD)
    kp = jnp.pad(k, ((0, 0), (WINDOW, 0), (0, 0), (0, 0)))
    vp = jnp.pad(v, ((0, 0), (WINDOW, 0), (0, 0), (0, 0)))
    t_idx = jnp.arange(WINDOW)[:, None]
    s_idx = jnp.arange(2 * WINDOW)[None, :]
    band = (s_idx > t_idx) & (s_idx <= t_idx + WINDOW)
    sink = sinks.astype(F32).reshape(C_KV_HEADS, C_GROUP)[None, :, :, None]
    scale = C_HD ** -0.5

    def block(args):
        j, qj = args
        kj = lax.dynamic_slice_in_dim(kp, j * WINDOW, 2 * WINDOW, axis=1)
        vj = lax.dynamic_slice_in_dim(vp, j * WINDOW, 2 * WINDOW, axis=1)
        valid = band & (j * WINDOW - WINDOW + s_idx >= 0)
        sc = jnp.einsum('btkgd,bskd->bkgts', qj, kj) * scale
        sc = jnp.where(valid, sc, -jnp.inf)
        m = jnp.maximum(jnp.max(sc, -1), sink)
        pr = jnp.exp(sc - m[..., None])
        den = jnp.sum(pr, -1) + jnp.exp(sink - m)
        pr = pr / den[..., None]
        return jnp.einsum('bkgts,bskd->btkgd', pr, vj)

    o = lax.map(block, (jnp.arange(nb), q))
    o = jnp.moveaxis(o, 0, 1).reshape(b, s, C_Q)
    return o.astype(hn.dtype) @ w_o + b_o


def setup_inputs(seed: int = 0) -> dict:
    key = jax.random.key(seed)
    ks = jax.random.split(key, 24)
    nrm = jax.random.normal
    x = nrm(ks[0], (BATCH, SEQ, D_MODEL), F32)
    p = nrm(ks[1], (DEPTH, BATCH, SEQ, PLE_DIM), F32)
    positions = jnp.broadcast_to(jnp.arange(SEQ, dtype=jnp.int32), (BATCH, SEQ))
    norm_gains = 1.0 + 0.05 * nrm(ks[2], (DEPTH, 4, D_MODEL), F32)
    w_in_ab = nrm(ks[3], (N_EVEN, D_MODEL, IN_COLS), F32) * D_MODEL ** -0.5
    conv_a = nrm(ks[4], (N_EVEN, CONV_K, A_CONV_CH), F32) * CONV_K ** -0.5
    a_log = jnp.log(jax.random.uniform(ks[5], (N_EVEN, A_HEADS), F32, 1.0, 16.0))
    dt = jnp.exp(jax.random.uniform(ks[6], (N_EVEN, A_HEADS), F32, np.log(1e-3), np.log(1e-1)))
    dt_bias = dt + jnp.log(-jnp.expm1(-dt))
    norm_a = 1.0 + 0.05 * nrm(ks[7], (N_EVEN, A_DV), F32)
    i_bias_b = 0.1 * nrm(ks[8], (N_EVEN, B_HEADS), F32)
    f_bias_b = 3.0 + 0.5 * nrm(ks[9], (N_EVEN, B_HEADS), F32)
    norm_b = 1.0 + 0.05 * nrm(ks[10], (N_EVEN, B_HEADS, B_DV), F32)
    w_out_ab = nrm(ks[11], (N_EVEN, MIX_AB, D_MODEL), F32) * MIX_AB ** -0.5
    w_qkv_c = nrm(ks[12], (N_ODD, D_MODEL, C_COLS), F32) * D_MODEL ** -0.5
    b_qkv_c = 0.02 * nrm(ks[13], (N_ODD, C_COLS), F32)
    sinks_c = nrm(ks[14], (N_ODD, C_HEADS), F32)
    w_o_c = nrm(ks[15], (N_ODD, C_Q, D_MODEL), F32) * C_Q ** -0.5
    b_o_c = 0.02 * nrm(ks[16], (N_ODD, D_MODEL), F32)
    w_up = nrm(ks[17], (DEPTH, D_MODEL, D_FF), F32) * D_MODEL ** -0.5
    w_down = nrm(ks[18], (DEPTH, D_FF, D_MODEL), F32) * D_FF ** -0.5
    w_ple = nrm(ks[19], (DEPTH, PLE_DIM, D_MODEL), F32) * PLE_DIM ** -0.5
    w_ple_gate = nrm(ks[20], (DEPTH, D_MODEL, D_MODEL), F32) * D_MODEL ** -0.5
    return {"x": x, "p": p, "positions": positions, "norm_gains": norm_gains,
            "w_in_ab": w_in_ab, "conv_a": conv_a, "a_log": a_log, "dt_bias": dt_bias,
            "norm_a": norm_a, "i_bias_b": i_bias_b, "f_bias_b": f_bias_b, "norm_b": norm_b,
            "w_out_ab": w_out_ab, "w_qkv_c": w_qkv_c, "b_qkv_c": b_qkv_c, "sinks_c": sinks_c,
            "w_o_c": w_o_c, "b_o_c": b_o_c, "w_up": w_up, "w_down": w_down,
            "w_ple": w_ple, "w_ple_gate": w_ple_gate}


def reference(x, p, positions, norm_gains, w_in_ab, conv_a, a_log, dt_bias, norm_a,
              i_bias_b, f_bias_b, norm_b, w_out_ab, w_qkv_c, b_qkv_c, sinks_c, w_o_c, b_o_c,
              w_up, w_down, w_ple, w_ple_gate):
    h = x
    for layer in range(DEPTH):
        gains = norm_gains[layer]
        hn = rmsnorm(h, gains[0])
        if layer % 2 == 0:
            e = layer // 2
            mix = mixer_ab(hn, w_in_ab[e], conv_a[e], a_log[e], dt_bias[e], norm_a[e],
                           i_bias_b[e], f_bias_b[e], norm_b[e], w_out_ab[e])
        else:
            o = layer // 2
            mix = mixer_c(hn, positions, w_qkv_c[o], b_qkv_c[o], sinks_c[o], w_o_c[o], b_o_c[o])
        h = h + rmsnorm(mix, gains[1])
        hn = rmsnorm(h, gains[2])
        ff = jnp.square(jax.nn.relu(hn @ w_up[layer])) @ w_down[layer]
        h = h + rmsnorm(ff, gains[3])
        gate = jax.nn.sigmoid(h @ w_ple_gate[layer])
        h = h + gate * (p[layer] @ w_ple[layer])
    return h
```

```python
import functools

import jax
import jax.numpy as jnp
from jax import lax
from jax.experimental import pallas as pl
from jax.experimental.pallas import tpu as pltpu

F32 = jnp.float32
BF16 = jnp.bfloat16
HIGHEST = lax.Precision.HIGHEST

EPS = 1e-6
D_MODEL = 1024
HEADS_AB = 4
HEAD_DIM_AB = 128
CONV_K = 4
CHUNK = 64
C_HEADS = 16
C_KV_HEADS = 4
C_GROUP = C_HEADS // C_KV_HEADS
C_HD = 64
WINDOW = 128
ROPE_THETA = 10000.0
D_FF = 4 * D_MODEL
PLE_DIM = 256
QK_AB = HEADS_AB * HEAD_DIM_AB
CONV_CH = 3 * QK_AB
BIG_COLS = 8 * QK_AB
GATE_COLS = 128
C_Q = C_HEADS * C_HD
C_KV = C_KV_HEADS * C_HD

SEQ_TILE = 512
POST_TILE = 512
FF_SPLIT = 4
VMEM_LIMIT_BYTES = 56 * 1024 * 1024


def _rms(x, g):
    return x * lax.rsqrt(jnp.mean(x * x, -1, keepdims=True) + EPS) * g


def _dot(a, b):
    return jnp.dot(a.astype(BF16), b.astype(BF16), preferred_element_type=F32)


def _dot_nt(a, b):
    return lax.dot_general(a.astype(BF16), b.astype(BF16), (((1,), (1,)), ((), ())),
                           preferred_element_type=F32)


def _dot_tn(a, b):
    return lax.dot_general(a.astype(BF16), b.astype(BF16), (((0,), (0,)), ((), ())),
                           preferred_element_type=F32)


def _dot_hi(a, b):
    return jnp.dot(a, b, preferred_element_type=F32, precision=HIGHEST)


def _softplus(x):
    return jnp.maximum(x, 0.0) + jnp.log1p(jnp.exp(-jnp.abs(x)))


def _sigmoid(x):
    return 1.0 / (1.0 + jnp.exp(-x))


def _silu(x):
    return x * _sigmoid(x)


def _mixer_ab_kernel(h_ref, g0_ref, wbig_ref, wgate_ref, convw_ref, gprm_ref, norma_ref,
                     normb_ref, out_ref,
                     proj_ref, xbuf_ref, qkv_ref, gate_ref, gcum_ref, gt_ref, gcumt_ref,
                     sa_ref, cb_ref, nb_ref, mb_ref):
    ts = h_ref.shape[0]
    n_chunks = ts // CHUNK
    t = pl.program_id(1)

    @pl.when(t == 0)
    def _():
        xbuf_ref[0:8, :] = jnp.zeros((8, CONV_CH), F32)
        sa_ref[...] = jnp.zeros_like(sa_ref)
        cb_ref[...] = jnp.zeros_like(cb_ref)
        nb_ref[...] = jnp.zeros_like(nb_ref)
        mb_ref[...] = jnp.zeros_like(mb_ref)

    hn = _rms(h_ref[...], g0_ref[...]).astype(BF16)
    proj_ref[...] = jnp.dot(hn, wbig_ref[...], preferred_element_type=F32)
    gates_pre = jnp.dot(hn, wgate_ref[...], preferred_element_type=F32) + gprm_ref[0:1, :]

    lane = lax.broadcasted_iota(jnp.int32, (1, GATE_COLS), 1)
    neg_a = -jnp.exp(gprm_ref[1:2, :])
    gate = jnp.where(lane < 4, _sigmoid(gates_pre),
                     jnp.where(lane < 8, neg_a * _softplus(gates_pre),
                               jnp.where(lane < 12, gates_pre, -_softplus(-gates_pre))))
    gate_ref[...] = gate
    row = lax.broadcasted_iota(jnp.int32, (CHUNK, CHUNK), 0)
    col = lax.broadcasted_iota(jnp.int32, (CHUNK, CHUNK), 1)
    causal = row >= col
    strict = row > col
    tri = causal.astype(F32)
    eye = (row == col).astype(F32)
    for c in range(n_chunks):
        gch = gate_ref[c * CHUNK:(c + 1) * CHUNK, :]
        cum = _dot_hi(tri, gch)
        gcum_ref[c * CHUNK:(c + 1) * CHUNK, :] = cum
        gt_ref[c] = gch.T[0:16, :]
        gcumt_ref[c] = cum.T[0:16, :]

    xbuf_ref[8:8 + ts, :] = proj_ref[:, 0:CONV_CH]
    conv = convw_ref[0:1, :] * xbuf_ref[5:5 + ts, :]
    for k in range(1, CONV_K):
        conv = conv + convw_ref[k:k + 1, :] * xbuf_ref[5 + k:5 + k + ts, :]
    xbuf_ref[0:8, :] = xbuf_ref[ts:ts + 8, :]
    qkv_ref[...] = _silu(conv)
    for hd in range(HEADS_AB):
        sl = slice(hd * HEAD_DIM_AB, (hd + 1) * HEAD_DIM_AB)
        q = qkv_ref[:, sl]
        qkv_ref[:, sl] = q * lax.rsqrt(jnp.sum(q * q, -1, keepdims=True) + EPS) * (HEAD_DIM_AB ** -0.5)
        sl = slice(QK_AB + hd * HEAD_DIM_AB, QK_AB + (hd + 1) * HEAD_DIM_AB)
        k = qkv_ref[:, sl]
        qkv_ref[:, sl] = k * lax.rsqrt(jnp.sum(k * k, -1, keepdims=True) + EPS)

    norm_a = norma_ref[...]

    def chunk_body(c, carry):
        r0 = pl.multiple_of(c * CHUNK, CHUNK)
        rows = pl.ds(r0, CHUNK)
        gt = gt_ref[c]
        gct = gcumt_ref[c]
        for hd in range(HEADS_AB):
            lo, hi = hd * HEAD_DIM_AB, (hd + 1) * HEAD_DIM_AB
            q = qkv_ref[rows, lo:hi]
            k = qkv_ref[rows, QK_AB + lo:QK_AB + hi]
            v = qkv_ref[rows, 2 * QK_AB + lo:2 * QK_AB + hi]
            beta = gate_ref[rows, hd:hd + 1]
            gc_col = gcum_ref[rows, 4 + hd:5 + hd]
            gc_row = gct[4 + hd:5 + hd, :]
            g_last = gcum_ref[pl.ds(r0 + CHUNK - 1, 1), 4 + hd:5 + hd]
            diff = gc_col - gc_row
            decay = jnp.where(causal, jnp.exp(jnp.where(causal, diff, 0.0)), 0.0)
            kb = k * beta
            vb = v * beta
            egc = jnp.exp(gc_col)
            lmat = jnp.where(strict, _dot_nt(kb, k) * decay, 0.0)
            tinv = eye - lmat
            pw = _dot_hi(lmat, lmat)
            for _ in range(4):
                tinv = tinv + _dot_hi(tinv, pw)
                pw = _dot_hi(pw, pw)
            tinv = tinv + _dot_hi(tinv, pw)
            u = _dot_hi(tinv, vb)
            w = _dot_hi(tinv, kb * egc)
            attn = _dot_nt(q, k) * decay
            q_dec = q * egc
            k_dec = k * jnp.exp(g_last - gc_col)
            state = sa_ref[hd]
            v_new = u - _dot(w, state)
            o = _dot(q_dec, state) + _dot(attn, v_new)
            sa_ref[hd] = state * jnp.exp(g_last) + _dot_tn(k_dec, v_new)
            z = proj_ref[rows, 3 * QK_AB + lo:3 * QK_AB + hi]
            out_ref[rows, lo:hi] = (_rms(o, norm_a) * _silu(z)).astype(out_ref.dtype)

            qb = proj_ref[rows, 4 * QK_AB + lo:4 * QK_AB + hi] * (HEAD_DIM_AB ** -0.5)
            kbb = proj_ref[rows, 5 * QK_AB + lo:5 * QK_AB + hi]
            vbb = proj_ref[rows, 6 * QK_AB + lo:6 * QK_AB + hi]
            i_col = gate_ref[rows, 8 + hd:9 + hd]
            i_row = gt[8 + hd:9 + hd, :]
            bc_col = gcum_ref[rows, 12 + hd:13 + hd]
            bc_row = gct[12 + hd:13 + hd, :]
            b_last = gcum_ref[pl.ds(r0 + CHUNK - 1, 1), 12 + hd:13 + hd]
            dmat = jnp.where(causal, bc_col - bc_row + i_row, -jnp.inf)
            m_intra = jnp.max(dmat, -1, keepdims=True)
            ws = b_last - bc_col + i_col
            qk = _dot_nt(qb, kbb)
            m_prev = mb_ref[hd][0:1, 0:1]
            c_st = cb_ref[hd]
            n_st = nb_ref[hd][0:1, :]
            a = bc_col + m_prev
            mt = jnp.maximum(a, m_intra)
            inter = jnp.exp(a - mt)
            pmat = qk * jnp.exp(dmat - mt)
            num = inter * _dot(qb, c_st) + _dot(pmat, vbb)
            den = inter * jnp.sum(qb * n_st, -1, keepdims=True) + jnp.sum(pmat, -1, keepdims=True)
            hout = num / jnp.maximum(jnp.abs(den), jnp.exp(-mt))
            m_new = jnp.maximum(b_last + m_prev, jnp.max(ws, 0, keepdims=True))
            sc = jnp.exp(b_last + m_prev - m_new)
            kw = kbb * jnp.exp(ws - m_new)
            cb_ref[hd] = sc * c_st + _dot_tn(kw, vbb)
            nb_ref[hd] = jnp.broadcast_to(sc * n_st + jnp.sum(kw, 0, keepdims=True), (8, HEAD_DIM_AB))
            mb_ref[hd] = jnp.broadcast_to(m_new, (8, 128))
            og = proj_ref[rows, 7 * QK_AB + lo:7 * QK_AB + hi]
            out_ref[rows, QK_AB + lo:QK_AB + hi] = (
                _rms(hout, normb_ref[hd:hd + 1, :]) * _sigmoid(og)).astype(out_ref.dtype)
        return carry

    lax.fori_loop(0, n_chunks, chunk_body, 0)


def _mixer_ab(h, g0, w_in, conv_w, a_log, dt_bias, norm_a, i_bias, f_bias, norm_b, batch, seq):
    tokens = batch * seq
    ts = min(SEQ_TILE, seq)
    tiles = seq // ts
    a_cols = 2 * QK_AB + QK_AB + 2 * HEADS_AB + QK_AB
    wa, wb = w_in[:, :a_cols], w_in[:, a_cols:]
    g_off = 3 * QK_AB
    z_off = g_off + 2 * HEADS_AB
    w_big = jnp.concatenate([wa[:, :g_off], wa[:, z_off:], wb[:, :g_off], wb[:, z_off:]], 1).astype(BF16)
    w_gate = jnp.concatenate([wa[:, g_off:z_off], wb[:, g_off:z_off],
                              jnp.zeros((D_MODEL, GATE_COLS - 4 * HEADS_AB), F32)], 1).astype(BF16)
    zeros4 = jnp.zeros((HEADS_AB,), F32)
    pad = jnp.zeros((GATE_COLS - 4 * HEADS_AB,), F32)
    bias_row = jnp.concatenate([zeros4, dt_bias, i_bias, f_bias, pad])
    alog_row = jnp.concatenate([zeros4, a_log, zeros4, zeros4, pad])
    gprm = jnp.concatenate([bias_row[None], alog_row[None], jnp.zeros((6, GATE_COLS), F32)], 0)

    const = lambda shape: pl.BlockSpec(shape, lambda b, t: (0,) * len(shape),
                                       pipeline_mode=pl.Buffered(1))
    n_chunks = ts // CHUNK
    return pl.pallas_call(
        _mixer_ab_kernel,
        grid=(batch, tiles),
        in_specs=[
            pl.BlockSpec((ts, D_MODEL), lambda b, t: (b * tiles + t, 0)),
            const((1, D_MODEL)),
            const((D_MODEL, BIG_COLS)),
            const((D_MODEL, GATE_COLS)),
            const((CONV_K, CONV_CH)),
            const((8, GATE_COLS)),
            const((1, HEAD_DIM_AB)),
            const((HEADS_AB, HEAD_DIM_AB)),
        ],
        out_specs=pl.BlockSpec((ts, 2 * QK_AB), lambda b, t: (b * tiles + t, 0)),
        out_shape=jax.ShapeDtypeStruct((tokens, 2 * QK_AB), BF16),
        scratch_shapes=[
            pltpu.VMEM((ts, BIG_COLS), F32),
            pltpu.VMEM((ts + 8, CONV_CH), F32),
            pltpu.VMEM((ts, CONV_CH), F32),
            pltpu.VMEM((ts, GATE_COLS), F32),
            pltpu.VMEM((ts, GATE_COLS), F32),
            pltpu.VMEM((n_chunks, 16, CHUNK), F32),
            pltpu.VMEM((n_chunks, 16, CHUNK), F32),
            pltpu.VMEM((HEADS_AB, HEAD_DIM_AB, HEAD_DIM_AB), F32),
            pltpu.VMEM((HEADS_AB, HEAD_DIM_AB, HEAD_DIM_AB), F32),
            pltpu.VMEM((HEADS_AB, 8, HEAD_DIM_AB), F32),
            pltpu.VMEM((HEADS_AB, 8, 128), F32),
        ],
        compiler_params=pltpu.CompilerParams(
            dimension_semantics=("arbitrary", "arbitrary"),
            vmem_limit_bytes=VMEM_LIMIT_BYTES),
        name="mixer_ab",
    )(h, g0[None], w_big, w_gate, conv_w, gprm, norm_a[None], norm_b)


def _mixer_c_kernel(sink_ref, h_ref, pos_ref, inv_ref, g0_ref, wqkv_ref, bqkv_ref, out_ref,
                    q_ref, k_ref, v_ref):
    ts = h_ref.shape[0]
    n_blocks = ts // WINDOW
    t = pl.program_id(1)

    @pl.when(t == 0)
    def _():
        k_ref[0:WINDOW, :] = jnp.zeros((WINDOW, C_KV), F32)
        v_ref[0:WINDOW, :] = jnp.zeros((WINDOW, C_KV), F32)

    hn = _rms(h_ref[...], g0_ref[...]).astype(BF16)
    proj = jnp.dot(hn, wqkv_ref[...], preferred_element_type=F32) + bqkv_ref[...]

    lane = lax.broadcasted_iota(jnp.int32, (1, 128), 1)
    first_half = (lane % C_HD) < (C_HD // 2)
    ang = pos_ref[...].astype(F32) * inv_ref[...]
    cos = jnp.cos(ang)
    sin = jnp.sin(ang)
    sin = jnp.where(first_half, -sin, sin)

    def rope(x):
        swapped = jnp.where(first_half, pltpu.roll(x, 128 - C_HD // 2, 1), pltpu.roll(x, C_HD // 2, 1))
        return x * cos + swapped * sin

    for j in range(C_Q // 128):
        q_ref[:, j * 128:(j + 1) * 128] = rope(proj[:, j * 128:(j + 1) * 128])
    for j in range(C_KV // 128):
        k_ref[WINDOW:, j * 128:(j + 1) * 128] = rope(proj[:, C_Q + j * 128:C_Q + (j + 1) * 128])
    v_ref[WINDOW:, :] = proj[:, C_Q + C_KV:]

    t_idx = lax.broadcasted_iota(jnp.int32, (WINDOW, 2 * WINDOW), 0)
    s_idx = lax.broadcasted_iota(jnp.int32, (WINDOW, 2 * WINDOW), 1)
    band = (s_idx > t_idx) & (s_idx <= t_idx + WINDOW)
    scale = C_HD ** -0.5
    for blk in range(n_blocks):
        if blk == 0:
            valid = band & (s_idx >= jnp.where(t > 0, 0, WINDOW))
        else:
            valid = band
        rows = slice(blk * WINDOW, (blk + 1) * WINDOW)
        krows = slice(blk * WINDOW, (blk + 2) * WINDOW)
        for g in range(C_KV_HEADS):
            kg = k_ref[krows, g * C_HD:(g + 1) * C_HD]
            vg = v_ref[krows, g * C_HD:(g + 1) * C_HD]
            for hh in range(C_GROUP):
                head = g * C_GROUP + hh
                qh = q_ref[rows, head * C_HD:(head + 1) * C_HD]
                sc = jnp.where(valid, _dot_nt(qh, kg) * scale, -jnp.inf)
                sink = sink_ref[head]
                m = jnp.maximum(jnp.max(sc, -1, keepdims=True), sink)
                pr = jnp.exp(sc - m)
                den = jnp.sum(pr, -1, keepdims=True) + jnp.exp(sink - m)
                pr = pr / den
                out_ref[rows, head * C_HD:(head + 1) * C_HD] = _dot(pr, vg).astype(out_ref.dtype)

    k_ref[0:WINDOW, :] = k_ref[ts:ts + WINDOW, :]
    v_ref[0:WINDOW, :] = v_ref[ts:ts + WINDOW, :]


def _mixer_c(h, g0, positions, w_qkv, b_qkv, sinks, batch, seq):
    tokens = batch * seq
    ts = min(SEQ_TILE, seq)
    tiles = seq // ts
    half = C_HD // 2
    inv = ROPE_THETA ** (-jnp.arange(half, dtype=F32) / half)
    inv_row = jnp.tile(inv, 128 // half)[None]
    pos = positions.reshape(tokens, 1)
    const = lambda shape: pl.BlockSpec(shape, lambda b, t, s: (0,) * len(shape),
                                       pipeline_mode=pl.Buffered(1))
    grid_spec = pltpu.PrefetchScalarGridSpec(
        num_scalar_prefetch=1,
        grid=(batch, tiles),
        in_specs=[
            pl.BlockSpec((ts, D_MODEL), lambda b, t, s: (b * tiles + t, 0)),
            pl.BlockSpec((ts, 1), lambda b, t, s: (b * tiles + t, 0)),
            const((1, 128)),
            const((1, D_MODEL)),
            const((D_MODEL, C_Q + 2 * C_KV)),
            const((1, C_Q + 2 * C_KV)),
        ],
        out_specs=pl.BlockSpec((ts, C_Q), lambda b, t, s: (b * tiles + t, 0)),
        scratch_shapes=[
            pltpu.VMEM((ts, C_Q), F32),
            pltpu.VMEM((ts + WINDOW, C_KV), F32),
            pltpu.VMEM((ts + WINDOW, C_KV), F32),
        ],
    )
    return pl.pallas_call(
        _mixer_c_kernel,
        grid_spec=grid_spec,
        out_shape=jax.ShapeDtypeStruct((tokens, C_Q), BF16),
        compiler_params=pltpu.CompilerParams(
            dimension_semantics=("arbitrary", "arbitrary"),
            vmem_limit_bytes=VMEM_LIMIT_BYTES),
        name="mixer_c",
    )(sinks, h, pos, inv_row, g0[None], w_qkv.astype(BF16), b_qkv[None])


def _post_kernel(mix_ref, h_ref, p_ref, wout_ref, bout_ref, gains_ref, wup_ref, wdown_ref,
                 wgate_ref, wple_ref, out_ref):
    mix = jnp.dot(mix_ref[...], wout_ref[...], preferred_element_type=F32) + bout_ref[...]
    h = h_ref[...] + _rms(mix, gains_ref[1:2, :])
    hn = _rms(h, gains_ref[2:3, :]).astype(BF16)
    slab = D_FF // FF_SPLIT
    ff = None
    for j in range(FF_SPLIT):
        up = jnp.dot(hn, wup_ref[:, j * slab:(j + 1) * slab], preferred_element_type=F32)
        act = jnp.square(jnp.maximum(up, 0.0)).astype(BF16)
        part = jnp.dot(act, wdown_ref[j * slab:(j + 1) * slab, :], preferred_element_type=F32)
        ff = part if ff is None else ff + part
    h = h + _rms(ff, gains_ref[3:4, :])
    gate = _sigmoid(jnp.dot(h.astype(BF16), wgate_ref[...], preferred_element_type=F32))
    ple = jnp.dot(p_ref[...].astype(BF16), wple_ref[...], preferred_element_type=F32)
    out_ref[...] = h + gate * ple


def _post(mixed, h, p_all, layer, w_out, b_out, gains, w_up, w_down, w_gate, w_ple):
    tokens = h.shape[0]
    tm = min(POST_TILE, tokens)
    const = lambda shape: pl.BlockSpec(shape, lambda i: (0,) * len(shape),
                                       pipeline_mode=pl.Buffered(1))
    return pl.pallas_call(
        _post_kernel,
        grid=(tokens // tm,),
        in_specs=[
            pl.BlockSpec((tm, D_MODEL), lambda i: (i, 0)),
            pl.BlockSpec((tm, D_MODEL), lambda i: (i, 0)),
            pl.BlockSpec((None, tm, PLE_DIM), lambda i: (layer, i, 0)),
            const((D_MODEL, D_MODEL)),
            const((1, D_MODEL)),
            const((4, D_MODEL)),
            const((D_MODEL, D_FF)),
            const((D_FF, D_MODEL)),
            const((D_MODEL, D_MODEL)),
            const((PLE_DIM, D_MODEL)),
        ],
        out_specs=pl.BlockSpec((tm, D_MODEL), lambda i: (i, 0)),
        out_shape=jax.ShapeDtypeStruct((tokens, D_MODEL), F32),
        compiler_params=pltpu.CompilerParams(
            dimension_semantics=("arbitrary",),
            vmem_limit_bytes=VMEM_LIMIT_BYTES),
        name="post",
    )(mixed, h, p_all, w_out.astype(BF16), b_out[None], gains, w_up.astype(BF16),
      w_down.astype(BF16), w_gate.astype(BF16), w_ple.astype(BF16))


def kernel(x, p, positions, norm_gains, w_in_ab, conv_a, a_log, dt_bias, norm_a, i_bias_b, f_bias_b, norm_b, w_out_ab, w_qkv_c, b_qkv_c, sinks_c, w_o_c, b_o_c, w_up, w_down, w_ple, w_ple_gate):
    batch, seq, _ = x.shape
    depth = norm_gains.shape[0]
    tokens = batch * seq
    h = x.reshape(tokens, D_MODEL)
    p_all = p.reshape(depth, tokens, PLE_DIM)
    for layer in range(depth):
        gains = norm_gains[layer]
        if layer % 2 == 0:
            e = layer // 2
            mixed = _mixer_ab(h, gains[0], w_in_ab[e], conv_a[e], a_log[e], dt_bias[e], norm_a[e],
                              i_bias_b[e], f_bias_b[e], norm_b[e], batch, seq)
            w_out, b_out = w_out_ab[e], jnp.zeros((D_MODEL,), F32)
        else:
            o = layer // 2
            mixed = _mixer_c(h, gains[0], positions, w_qkv_c[o], b_qkv_c[o], sinks_c[o], batch, seq)
            w_out, b_out = w_o_c[o], b_o_c[o]
        h = _post(mixed, h, p_all, layer, w_out, b_out, gains, w_up[layer], w_down[layer],
                  w_ple_gate[layer], w_ple[layer])
    return h.reshape(batch, seq, D_MODEL)
```

```python
import functools

import jax
import jax.numpy as jnp
from jax import lax
from jax.experimental import pallas as pl
from jax.experimental.pallas import tpu as pltpu

F32 = jnp.float32
BF16 = jnp.bfloat16
HIGHEST = lax.Precision.HIGHEST

EPS = 1e-6
D_MODEL = 1024
HEADS_AB = 4
HEAD_DIM_AB = 128
CONV_K = 4
CHUNK = 64
C_HEADS = 16
C_KV_HEADS = 4
C_GROUP = C_HEADS // C_KV_HEADS
C_HD = 64
WINDOW = 128
ROPE_THETA = 10000.0
D_FF = 4 * D_MODEL
PLE_DIM = 256
QK_AB = HEADS_AB * HEAD_DIM_AB
CONV_CH = 3 * QK_AB
BIG_COLS = 8 * QK_AB
GATE_COLS = 128
C_Q = C_HEADS * C_HD
C_KV = C_KV_HEADS * C_HD

SEQ_TILE = 512
POST_TILE = 512
PREP_CHUNKS = 2
FF_SPLIT = 4
VMEM_LIMIT_BYTES = 56 * 1024 * 1024


def _rms(x, g):
    return x * lax.rsqrt(jnp.mean(x * x, -1, keepdims=True) + EPS) * g


def _dot(a, b):
    return jnp.dot(a.astype(BF16), b.astype(BF16), preferred_element_type=F32)


def _dot_nt(a, b):
    return lax.dot_general(a.astype(BF16), b.astype(BF16), (((1,), (1,)), ((), ())),
                           preferred_element_type=F32)


def _dot_tn(a, b):
    return lax.dot_general(a.astype(BF16), b.astype(BF16), (((0,), (0,)), ((), ())),
                           preferred_element_type=F32)


def _dot_hi(a, b):
    return jnp.dot(a, b, preferred_element_type=F32, precision=HIGHEST)


def _split(a):
    hi = a.astype(BF16)
    lo = (a - hi.astype(F32)).astype(BF16)
    return hi, lo


def _dot3(a, b):
    (ah, al), (bh, bl) = a, b
    d = lambda x, y: lax.dot_general(x, y, (((2,), (1,)), ((0,), (0,))), preferred_element_type=F32)
    return d(ah, bl) + d(al, bh) + d(ah, bh)


def _softplus(x):
    return jnp.maximum(x, 0.0) + jnp.log1p(jnp.exp(-jnp.abs(x)))


def _sigmoid(x):
    return 1.0 / (1.0 + jnp.exp(-x))


def _silu(x):
    return x * _sigmoid(x)


def _mixer_ab_kernel(h_ref, g0_ref, wbig_ref, wgate_ref, convw_ref, gprm_ref, norma_ref,
                     normb_ref, out_ref,
                     proj_ref, xbuf_ref, qkv_ref, gate_ref, gcum_ref, gt_ref, gcumt_ref,
                     u_ref, wq_ref, attn_ref, kdect_ref, qk_ref, dmat_ref, mintra_ref, kbt_ref,
                     sa_ref, cb_ref, nb_ref, mb_ref):
    ts = h_ref.shape[0]
    n_chunks = ts // CHUNK
    t = pl.program_id(1)

    @pl.when(t == 0)
    def _():
        xbuf_ref[0:8, :] = jnp.zeros((8, CONV_CH), F32)
        sa_ref[...] = jnp.zeros_like(sa_ref)
        cb_ref[...] = jnp.zeros_like(cb_ref)
        nb_ref[...] = jnp.zeros_like(nb_ref)
        mb_ref[...] = jnp.zeros_like(mb_ref)

    hn = _rms(h_ref[...], g0_ref[...]).astype(BF16)
    proj_ref[...] = jnp.dot(hn, wbig_ref[...], preferred_element_type=F32)
    gates_pre = jnp.dot(hn, wgate_ref[...], preferred_element_type=F32) + gprm_ref[0:1, :]

    lane = lax.broadcasted_iota(jnp.int32, (1, GATE_COLS), 1)
    neg_a = -jnp.exp(gprm_ref[1:2, :])
    gate = jnp.where(lane < 4, _sigmoid(gates_pre),
                     jnp.where(lane < 8, neg_a * _softplus(gates_pre),
                               jnp.where(lane < 12, gates_pre, -_softplus(-gates_pre))))
    gate_ref[...] = gate
    row = lax.broadcasted_iota(jnp.int32, (CHUNK, CHUNK), 0)
    col = lax.broadcasted_iota(jnp.int32, (CHUNK, CHUNK), 1)
    causal = row >= col
    strict = row > col
    tri = causal.astype(F32)
    eye = (row == col).astype(F32)
    for c in range(n_chunks):
        gch = gate_ref[c * CHUNK:(c + 1) * CHUNK, :]
        cum = _dot_hi(tri, gch)
        gcum_ref[c * CHUNK:(c + 1) * CHUNK, :] = cum
        gt_ref[c] = gch.T[0:16, :]
        gcumt_ref[c] = cum.T[0:16, :]

    xbuf_ref[8:8 + ts, :] = proj_ref[:, 0:CONV_CH]
    conv = convw_ref[0:1, :] * xbuf_ref[5:5 + ts, :]
    for k in range(1, CONV_K):
        conv = conv + convw_ref[k:k + 1, :] * xbuf_ref[5 + k:5 + k + ts, :]
    xbuf_ref[0:8, :] = xbuf_ref[ts:ts + 8, :]
    qkv_ref[...] = _silu(conv)
    for hd in range(HEADS_AB):
        sl = slice(hd * HEAD_DIM_AB, (hd + 1) * HEAD_DIM_AB)
        q = qkv_ref[:, sl]
        qkv_ref[:, sl] = q * lax.rsqrt(jnp.sum(q * q, -1, keepdims=True) + EPS) * (HEAD_DIM_AB ** -0.5)
        sl = slice(QK_AB + hd * HEAD_DIM_AB, QK_AB + (hd + 1) * HEAD_DIM_AB)
        k = qkv_ref[:, sl]
        qkv_ref[:, sl] = k * lax.rsqrt(jnp.sum(k * k, -1, keepdims=True) + EPS)

    norm_a = norma_ref[...]

    def prep_body(i, carry):
        lmats, rhs = [], []
        for j in range(PREP_CHUNKS):
            c = i * PREP_CHUNKS + j
            r0 = pl.multiple_of(c * CHUNK, CHUNK)
            rows = pl.ds(r0, CHUNK)
            gct = gcumt_ref[c]
            for hd in range(HEADS_AB):
                lo, hi = hd * HEAD_DIM_AB, (hd + 1) * HEAD_DIM_AB
                q = qkv_ref[rows, lo:hi]
                k = qkv_ref[rows, QK_AB + lo:QK_AB + hi]
                v = qkv_ref[rows, 2 * QK_AB + lo:2 * QK_AB + hi]
                beta = gate_ref[rows, hd:hd + 1]
                gc_col = gcum_ref[rows, 4 + hd:5 + hd]
                gc_row = gct[4 + hd:5 + hd, :]
                g_last = gcum_ref[pl.ds(r0 + CHUNK - 1, 1), 4 + hd:5 + hd]
                diff = gc_col - gc_row
                decay = jnp.where(causal, jnp.exp(jnp.where(causal, diff, 0.0)), 0.0)
                kb = k * beta
                egc = jnp.exp(gc_col)
                k_bf = k.astype(BF16)
                lmats.append(jnp.where(strict, _dot_nt(kb, k_bf) * decay, 0.0))
                rhs.append((c, hd, rows, (v * beta).astype(BF16), (kb * egc).astype(BF16)))
                wq_ref[c, hd, CHUNK:2 * CHUNK, :] = (q * egc).astype(BF16)
                attn_ref[c, hd] = (_dot_nt(q, k_bf) * decay).astype(BF16)
                kdect_ref[c, hd] = (k * jnp.exp(g_last - gc_col)).T.astype(BF16)
                qb = proj_ref[rows, 4 * QK_AB + lo:4 * QK_AB + hi] * (HEAD_DIM_AB ** -0.5)
                kbb = proj_ref[rows, 5 * QK_AB + lo:5 * QK_AB + hi]
                bc_col = gcum_ref[rows, 12 + hd:13 + hd]
                dmat = jnp.where(causal, bc_col - gct[12 + hd:13 + hd, :] + gt_ref[c][8 + hd:9 + hd, :],
                                 -jnp.inf)
                dmat_ref[c, hd] = dmat
                mintra_ref[c, hd] = jnp.max(dmat, -1, keepdims=True)
                qk_ref[c, hd] = _dot_nt(qb, kbb)
                kbt_ref[c, hd] = kbb.T
        lmat = jnp.stack(lmats)
        l_split = _split(lmat)
        tinv = eye - lmat
        pw = _dot3(l_split, l_split)
        for _ in range(4):
            pw_split = _split(pw)
            tinv = tinv + _dot3(_split(tinv), pw_split)
            pw = _dot3(pw_split, pw_split)
        tinv = (tinv + _dot3(_split(tinv), _split(pw))).astype(BF16)
        for n, (c, hd, rows, vb, kbe) in enumerate(rhs):
            lo, hi = hd * HEAD_DIM_AB, (hd + 1) * HEAD_DIM_AB
            u_ref[rows, lo:hi] = jnp.dot(tinv[n], vb, preferred_element_type=F32)
            wq_ref[c, hd, 0:CHUNK, :] = jnp.dot(tinv[n], kbe, preferred_element_type=F32).astype(BF16)
        return carry

    lax.fori_loop(0, n_chunks // PREP_CHUNKS, prep_body, 0)

    def chunk_body(c, carry):
        r0 = pl.multiple_of(c * CHUNK, CHUNK)
        rows = pl.ds(r0, CHUNK)
        last = pl.ds(r0 + CHUNK - 1, 1)
        heads = range(HEADS_AB)
        col = lambda ref, off: jnp.stack([ref[rows, off + hd:off + hd + 1] for hd in heads])
        end = lambda off: jnp.stack([gcum_ref[last, off + hd:off + hd + 1] for hd in heads])
        wide = lambda ref, off: jnp.stack(
            [ref[rows, off + hd * HEAD_DIM_AB:off + (hd + 1) * HEAD_DIM_AB] for hd in heads])
        bdot = lambda x, y: lax.dot_general(x, y, (((2,), (1,)), ((0,), (0,))),
                                            preferred_element_type=F32)
        state = sa_ref[...]
        ws_qs = bdot(wq_ref[c], state.astype(BF16))
        qb = wide(proj_ref, 4 * QK_AB) * (HEAD_DIM_AB ** -0.5)
        qb_bf = qb.astype(BF16)
        c_st = cb_ref[...]
        q_c = bdot(qb_bf, c_st.astype(BF16))

        v_new = (wide(u_ref, 0) - ws_qs[:, 0:CHUNK]).astype(BF16)
        o = ws_qs[:, CHUNK:] + bdot(attn_ref[c], v_new)
        sa_ref[...] = state * jnp.exp(end(4)) + bdot(kdect_ref[c], v_new)

        vbb = wide(proj_ref, 6 * QK_AB).astype(BF16)
        kbb = wide(proj_ref, 5 * QK_AB)
        m_prev = mb_ref[:, 0:1, 0:1]
        n_st = nb_ref[:, 0:1, :]
        b_last = end(12)
        bc_col = col(gcum_ref, 12)
        a = bc_col + m_prev
        mt = jnp.maximum(a, mintra_ref[c])
        inter = jnp.exp(a - mt)
        pmat = qk_ref[c] * jnp.exp(dmat_ref[c] - mt)
        num = inter * q_c + bdot(pmat.astype(BF16), vbb)
        den = inter * jnp.sum(qb * n_st, -1, keepdims=True) + jnp.sum(pmat, -1, keepdims=True)
        hout = num / jnp.maximum(jnp.abs(den), jnp.exp(-mt))
        ws_col = b_last - bc_col + col(gate_ref, 8)
        ws_row = (b_last - jnp.stack([gcumt_ref[c][12 + hd:13 + hd, :] for hd in heads])
                  + jnp.stack([gt_ref[c][8 + hd:9 + hd, :] for hd in heads]))
        m_new = jnp.maximum(b_last + m_prev, jnp.max(ws_row, -1, keepdims=True))
        sc = jnp.exp(b_last + m_prev - m_new)
        kw_t = (kbt_ref[c] * jnp.exp(ws_row - m_new)).astype(BF16)
        cb_ref[...] = sc * c_st + bdot(kw_t, vbb)
        n_new = sc * n_st + jnp.sum(kbb * jnp.exp(ws_col - m_new), 1, keepdims=True)
        nb_ref[...] = jnp.broadcast_to(n_new, nb_ref.shape)
        mb_ref[...] = jnp.broadcast_to(m_new, mb_ref.shape)

        for hd in heads:
            lo, hi = hd * HEAD_DIM_AB, (hd + 1) * HEAD_DIM_AB
            z = proj_ref[rows, 3 * QK_AB + lo:3 * QK_AB + hi]
            out_ref[rows, lo:hi] = (_rms(o[hd], norm_a) * _silu(z)).astype(out_ref.dtype)
            og = proj_ref[rows, 7 * QK_AB + lo:7 * QK_AB + hi]
            out_ref[rows, QK_AB + lo:QK_AB + hi] = (
                _rms(hout[hd], normb_ref[hd:hd + 1, :]) * _sigmoid(og)).astype(out_ref.dtype)
        return carry

    lax.fori_loop(0, n_chunks, chunk_body, 0)


def _mixer_ab(h, g0, w_in, conv_w, a_log, dt_bias, norm_a, i_bias, f_bias, norm_b, batch, seq):
    tokens = batch * seq
    ts = min(SEQ_TILE, seq)
    tiles = seq // ts
    a_cols = 2 * QK_AB + QK_AB + 2 * HEADS_AB + QK_AB
    wa, wb = w_in[:, :a_cols], w_in[:, a_cols:]
    g_off = 3 * QK_AB
    z_off = g_off + 2 * HEADS_AB
    w_big = jnp.concatenate([wa[:, :g_off], wa[:, z_off:], wb[:, :g_off], wb[:, z_off:]], 1).astype(BF16)
    w_gate = jnp.concatenate([wa[:, g_off:z_off], wb[:, g_off:z_off],
                              jnp.zeros((D_MODEL, GATE_COLS - 4 * HEADS_AB), F32)], 1).astype(BF16)
    zeros4 = jnp.zeros((HEADS_AB,), F32)
    pad = jnp.zeros((GATE_COLS - 4 * HEADS_AB,), F32)
    bias_row = jnp.concatenate([zeros4, dt_bias, i_bias, f_bias, pad])
    alog_row = jnp.concatenate([zeros4, a_log, zeros4, zeros4, pad])
    gprm = jnp.concatenate([bias_row[None], alog_row[None], jnp.zeros((6, GATE_COLS), F32)], 0)

    const = lambda shape: pl.BlockSpec(shape, lambda b, t: (0,) * len(shape),
                                       pipeline_mode=pl.Buffered(1))
    n_chunks = ts // CHUNK
    return pl.pallas_call(
        _mixer_ab_kernel,
        grid=(batch, tiles),
        in_specs=[
            pl.BlockSpec((ts, D_MODEL), lambda b, t: (b * tiles + t, 0)),
            const((1, D_MODEL)),
            const((D_MODEL, BIG_COLS)),
            const((D_MODEL, GATE_COLS)),
            const((CONV_K, CONV_CH)),
            const((8, GATE_COLS)),
            const((1, HEAD_DIM_AB)),
            const((HEADS_AB, HEAD_DIM_AB)),
        ],
        out_specs=pl.BlockSpec((ts, 2 * QK_AB), lambda b, t: (b * tiles + t, 0)),
        out_shape=jax.ShapeDtypeStruct((tokens, 2 * QK_AB), BF16),
        scratch_shapes=[
            pltpu.VMEM((ts, BIG_COLS), F32),
            pltpu.VMEM((ts + 8, CONV_CH), F32),
            pltpu.VMEM((ts, CONV_CH), F32),
            pltpu.VMEM((ts, GATE_COLS), F32),
            pltpu.VMEM((ts, GATE_COLS), F32),
            pltpu.VMEM((n_chunks, 16, CHUNK), F32),
            pltpu.VMEM((n_chunks, 16, CHUNK), F32),
            pltpu.VMEM((ts, QK_AB), F32),
            pltpu.VMEM((n_chunks, HEADS_AB, 2 * CHUNK, HEAD_DIM_AB), BF16),
            pltpu.VMEM((n_chunks, HEADS_AB, CHUNK, CHUNK), BF16),
            pltpu.VMEM((n_chunks, HEADS_AB, HEAD_DIM_AB, CHUNK), BF16),
            pltpu.VMEM((n_chunks, HEADS_AB, CHUNK, CHUNK), F32),
            pltpu.VMEM((n_chunks, HEADS_AB, CHUNK, CHUNK), F32),
            pltpu.VMEM((n_chunks, HEADS_AB, CHUNK, 1), F32),
            pltpu.VMEM((n_chunks, HEADS_AB, HEAD_DIM_AB, CHUNK), F32),
            pltpu.VMEM((HEADS_AB, HEAD_DIM_AB, HEAD_DIM_AB), F32),
            pltpu.VMEM((HEADS_AB, HEAD_DIM_AB, HEAD_DIM_AB), F32),
            pltpu.VMEM((HEADS_AB, 8, HEAD_DIM_AB), F32),
            pltpu.VMEM((HEADS_AB, 8, 128), F32),
        ],
        compiler_params=pltpu.CompilerParams(
            dimension_semantics=("arbitrary", "arbitrary"),
            vmem_limit_bytes=VMEM_LIMIT_BYTES),
        name="mixer_ab",
    )(h, g0[None], w_big, w_gate, conv_w, gprm, norm_a[None], norm_b)


def _mixer_c_kernel(sink_ref, h_ref, pos_ref, inv_ref, g0_ref, wqkv_ref, bqkv_ref, out_ref,
                    q_ref, k_ref, v_ref):
    ts = h_ref.shape[0]
    n_blocks = ts // WINDOW
    t = pl.program_id(1)

    @pl.when(t == 0)
    def _():
        k_ref[0:WINDOW, :] = jnp.zeros((WINDOW, C_KV), F32)
        v_ref[0:WINDOW, :] = jnp.zeros((WINDOW, C_KV), F32)

    hn = _rms(h_ref[...], g0_ref[...]).astype(BF16)
    proj = jnp.dot(hn, wqkv_ref[...], preferred_element_type=F32) + bqkv_ref[...]

    lane = lax.broadcasted_iota(jnp.int32, (1, 128), 1)
    first_half = (lane % C_HD) < (C_HD // 2)
    ang = pos_ref[...].astype(F32) * inv_ref[...]
    cos = jnp.cos(ang)
    sin = jnp.sin(ang)
    sin = jnp.where(first_half, -sin, sin)

    def rope(x):
        swapped = jnp.where(first_half, pltpu.roll(x, 128 - C_HD // 2, 1), pltpu.roll(x, C_HD // 2, 1))
        return x * cos + swapped * sin

    for j in range(C_Q // 128):
        q_ref[:, j * 128:(j + 1) * 128] = rope(proj[:, j * 128:(j + 1) * 128])
    for j in range(C_KV // 128):
        k_ref[WINDOW:, j * 128:(j + 1) * 128] = rope(proj[:, C_Q + j * 128:C_Q + (j + 1) * 128])
    v_ref[WINDOW:, :] = proj[:, C_Q + C_KV:]

    t_idx = lax.broadcasted_iota(jnp.int32, (WINDOW, 2 * WINDOW), 0)
    s_idx = lax.broadcasted_iota(jnp.int32, (WINDOW, 2 * WINDOW), 1)
    band = (s_idx > t_idx) & (s_idx <= t_idx + WINDOW)
    scale = C_HD ** -0.5
    for blk in range(n_blocks):
        if blk == 0:
            valid = band & (s_idx >= jnp.where(t > 0, 0, WINDOW))
        else:
            valid = band
        rows = slice(blk * WINDOW, (blk + 1) * WINDOW)
        krows = slice(blk * WINDOW, (blk + 2) * WINDOW)
        for g in range(C_KV_HEADS):
            kg = k_ref[krows, g * C_HD:(g + 1) * C_HD]
            vg = v_ref[krows, g * C_HD:(g + 1) * C_HD]
            for hh in range(C_GROUP):
                head = g * C_GROUP + hh
                qh = q_ref[rows, head * C_HD:(head + 1) * C_HD]
                sc = jnp.where(valid, _dot_nt(qh, kg) * scale, -jnp.inf)
                sink = sink_ref[head]
                m = jnp.maximum(jnp.max(sc, -1, keepdims=True), sink)
                pr = jnp.exp(sc - m)
                den = jnp.sum(pr, -1, keepdims=True) + jnp.exp(sink - m)
                pr = pr / den
                out_ref[rows, head * C_HD:(head + 1) * C_HD] = _dot(pr, vg).astype(out_ref.dtype)

    k_ref[0:WINDOW, :] = k_ref[ts:ts + WINDOW, :]
    v_ref[0:WINDOW, :] = v_ref[ts:ts + WINDOW, :]


def _mixer_c(h, g0, positions, w_qkv, b_qkv, sinks, batch, seq):
    tokens = batch * seq
    ts = min(SEQ_TILE, seq)
    tiles = seq // ts
    half = C_HD // 2
    inv = ROPE_THETA ** (-jnp.arange(half, dtype=F32) / half)
    inv_row = jnp.tile(inv, 128 // half)[None]
    pos = positions.reshape(tokens, 1)
    const = lambda shape: pl.BlockSpec(shape, lambda b, t, s: (0,) * len(shape),
                                       pipeline_mode=pl.Buffered(1))
    grid_spec = pltpu.PrefetchScalarGridSpec(
        num_scalar_prefetch=1,
        grid=(batch, tiles),
        in_specs=[
            pl.BlockSpec((ts, D_MODEL), lambda b, t, s: (b * tiles + t, 0)),
            pl.BlockSpec((ts, 1), lambda b, t, s: (b * tiles + t, 0)),
            const((1, 128)),
            const((1, D_MODEL)),
            const((D_MODEL, C_Q + 2 * C_KV)),
            const((1, C_Q + 2 * C_KV)),
        ],
        out_specs=pl.BlockSpec((ts, C_Q), lambda b, t, s: (b * tiles + t, 0)),
        scratch_shapes=[
            pltpu.VMEM((ts, C_Q), F32),
            pltpu.VMEM((ts + WINDOW, C_KV), F32),
            pltpu.VMEM((ts + WINDOW, C_KV), F32),
        ],
    )
    return pl.pallas_call(
        _mixer_c_kernel,
        grid_spec=grid_spec,
        out_shape=jax.ShapeDtypeStruct((tokens, C_Q), BF16),
        compiler_params=pltpu.CompilerParams(
            dimension_semantics=("arbitrary", "arbitrary"),
            vmem_limit_bytes=VMEM_LIMIT_BYTES),
        name="mixer_c",
    )(sinks, h, pos, inv_row, g0[None], w_qkv.astype(BF16), b_qkv[None])


def _post_kernel(mix_ref, h_ref, p_ref, wout_ref, bout_ref, gains_ref, wup_ref, wdown_ref,
                 wgate_ref, wple_ref, out_ref):
    mix = jnp.dot(mix_ref[...], wout_ref[...], preferred_element_type=F32) + bout_ref[...]
    h = h_ref[...] + _rms(mix, gains_ref[1:2, :])
    hn = _rms(h, gains_ref[2:3, :]).astype(BF16)
    slab = D_FF // FF_SPLIT
    ff = None
    for j in range(FF_SPLIT):
        up = jnp.dot(hn, wup_ref[:, j * slab:(j + 1) * slab], preferred_element_type=F32)
        act = jnp.square(jnp.maximum(up, 0.0)).astype(BF16)
        part = jnp.dot(act, wdown_ref[j * slab:(j + 1) * slab, :], preferred_element_type=F32)
        ff = part if ff is None else ff + part
    h = h + _rms(ff, gains_ref[3:4, :])
    gate = _sigmoid(jnp.dot(h.astype(BF16), wgate_ref[...], preferred_element_type=F32))
    ple = jnp.dot(p_ref[...].astype(BF16), wple_ref[...], preferred_element_type=F32)
    out_ref[...] = h + gate * ple


def _post(mixed, h, p_all, layer, w_out, b_out, gains, w_up, w_down, w_gate, w_ple):
    tokens = h.shape[0]
    tm = min(POST_TILE, tokens)
    const = lambda shape: pl.BlockSpec(shape, lambda i: (0,) * len(shape),
                                       pipeline_mode=pl.Buffered(1))
    return pl.pallas_call(
        _post_kernel,
        grid=(tokens // tm,),
        in_specs=[
            pl.BlockSpec((tm, D_MODEL), lambda i: (i, 0)),
            pl.BlockSpec((tm, D_MODEL), lambda i: (i, 0)),
            pl.BlockSpec((None, tm, PLE_DIM), lambda i: (layer, i, 0)),
            const((D_MODEL, D_MODEL)),
            const((1, D_MODEL)),
            const((4, D_MODEL)),
            const((D_MODEL, D_FF)),
            const((D_FF, D_MODEL)),
            const((D_MODEL, D_MODEL)),
            const((PLE_DIM, D_MODEL)),
        ],
        out_specs=pl.BlockSpec((tm, D_MODEL), lambda i: (i, 0)),
        out_shape=jax.ShapeDtypeStruct((tokens, D_MODEL), F32),
        compiler_params=pltpu.CompilerParams(
            dimension_semantics=("arbitrary",),
            vmem_limit_bytes=VMEM_LIMIT_BYTES),
        name="post",
    )(mixed, h, p_all, w_out.astype(BF16), b_out[None], gains, w_up.astype(BF16),
      w_down.astype(BF16), w_gate.astype(BF16), w_ple.astype(BF16))


def kernel(x, p, positions, norm_gains, w_in_ab, conv_a, a_log, dt_bias, norm_a, i_bias_b, f_bias_b, norm_b, w_out_ab, w_qkv_c, b_qkv_c, sinks_c, w_o_c, b_o_c, w_up, w_down, w_ple, w_ple_gate):
    batch, seq, _ = x.shape
    depth = norm_gains.shape[0]
    tokens = batch * seq
    h = x.reshape(tokens, D_MODEL)
    p_all = p.reshape(depth, tokens, PLE_DIM)
    for layer in range(depth):
        gains = norm_gains[layer]
        if layer % 2 == 0:
            e = layer // 2
            mixed = _mixer_ab(h, gains[0], w_in_ab[e], conv_a[e], a_log[e], dt_bias[e], norm_a[e],
                              i_bias_b[e], f_bias_b[e], norm_b[e], batch, seq)
            w_out, b_out = w_out_ab[e], jnp.zeros((D_MODEL,), F32)
        else:
            o = layer // 2
            mixed = _mixer_c(h, gains[0], positions, w_qkv_c[o], b_qkv_c[o], sinks_c[o], batch, seq)
            w_out, b_out = w_o_c[o], b_o_c[o]
        h = _post(mixed, h, p_all, layer, w_out, b_out, gains, w_up[layer], w_down[layer],
                  w_ple_gate[layer], w_ple[layer])
    return h.reshape(batch, seq, D_MODEL)
```

```python
import functools

import jax
import jax.numpy as jnp
from jax import lax
from jax.experimental import pallas as pl
from jax.experimental.pallas import tpu as pltpu

F32 = jnp.float32
BF16 = jnp.bfloat16
HIGHEST = lax.Precision.HIGHEST

EPS = 1e-6
D_MODEL = 1024
HEADS_AB = 4
HEAD_DIM_AB = 128
CONV_K = 4
CHUNK = 64
C_HEADS = 16
C_KV_HEADS = 4
C_GROUP = C_HEADS // C_KV_HEADS
C_HD = 64
WINDOW = 128
ROPE_THETA = 10000.0
D_FF = 4 * D_MODEL
PLE_DIM = 256
QK_AB = HEADS_AB * HEAD_DIM_AB
CONV_CH = 3 * QK_AB
BIG_COLS = 8 * QK_AB
GATE_COLS = 128
C_Q = C_HEADS * C_HD
C_KV = C_KV_HEADS * C_HD

SEQ_TILE = 512
POST_TILE = 512
PREP_CHUNKS = 2
FF_SPLIT = 4
VMEM_LIMIT_BYTES = 56 * 1024 * 1024


def _rms(x, g):
    return x * lax.rsqrt(jnp.mean(x * x, -1, keepdims=True) + EPS) * g


def _dot(a, b):
    return jnp.dot(a.astype(BF16), b.astype(BF16), preferred_element_type=F32)


def _dot_nt(a, b):
    return lax.dot_general(a.astype(BF16), b.astype(BF16), (((1,), (1,)), ((), ())),
                           preferred_element_type=F32)


def _dot_tn(a, b):
    return lax.dot_general(a.astype(BF16), b.astype(BF16), (((0,), (0,)), ((), ())),
                           preferred_element_type=F32)


def _dot_hi(a, b):
    return jnp.dot(a, b, preferred_element_type=F32, precision=HIGHEST)


def _split(a):
    hi = a.astype(BF16)
    lo = (a - hi.astype(F32)).astype(BF16)
    return hi, lo


def _dot3(a, b):
    (ah, al), (bh, bl) = a, b
    d = lambda x, y: lax.dot_general(x, y, (((2,), (1,)), ((0,), (0,))), preferred_element_type=F32)
    return d(ah, bl) + d(al, bh) + d(ah, bh)


def _softplus(x):
    return jnp.maximum(x, 0.0) + jnp.log1p(jnp.exp(-jnp.abs(x)))


def _sigmoid(x):
    return 1.0 / (1.0 + jnp.exp(-x))


def _silu(x):
    return x * _sigmoid(x)


def _mixer_ab_kernel(h_ref, g0_ref, wbig_ref, wgate_ref, convw_ref, gprm_ref, norma_ref,
                     normb_ref, out_ref,
                     proj_ref, xbuf_ref, qkv_ref, gate_ref, gcum_ref, gt_ref, gcumt_ref,
                     u_ref, wq_ref, attn_ref, kdect_ref, qk_ref, dmat_ref, mintra_ref, kbt_ref,
                     sa_ref, cb_ref, nb_ref, mb_ref):
    ts = h_ref.shape[0]
    n_chunks = ts // CHUNK
    t = pl.program_id(1)

    @pl.when(t == 0)
    def _():
        xbuf_ref[0:8, :] = jnp.zeros((8, CONV_CH), F32)
        sa_ref[...] = jnp.zeros_like(sa_ref)
        cb_ref[...] = jnp.zeros_like(cb_ref)
        nb_ref[...] = jnp.zeros_like(nb_ref)
        mb_ref[...] = jnp.zeros_like(mb_ref)

    hn = _rms(h_ref[...], g0_ref[...]).astype(BF16)
    proj_ref[...] = jnp.dot(hn, wbig_ref[...], preferred_element_type=F32)
    gates_pre = jnp.dot(hn, wgate_ref[...], preferred_element_type=F32) + gprm_ref[0:1, :]

    lane = lax.broadcasted_iota(jnp.int32, (1, GATE_COLS), 1)
    neg_a = -jnp.exp(gprm_ref[1:2, :])
    gate = jnp.where(lane < 4, _sigmoid(gates_pre),
                     jnp.where(lane < 8, neg_a * _softplus(gates_pre),
                               jnp.where(lane < 12, gates_pre, -_softplus(-gates_pre))))
    gate_ref[...] = gate
    row = lax.broadcasted_iota(jnp.int32, (CHUNK, CHUNK), 0)
    col = lax.broadcasted_iota(jnp.int32, (CHUNK, CHUNK), 1)
    causal = row >= col
    strict = row > col
    tri = causal.astype(F32)
    eye = (row == col).astype(F32)
    for c in range(n_chunks):
        gch = gate_ref[c * CHUNK:(c + 1) * CHUNK, :]
        cum = _dot_hi(tri, gch)
        gcum_ref[c * CHUNK:(c + 1) * CHUNK, :] = cum
        gt_ref[c] = gch.T[0:16, :]
        gcumt_ref[c] = cum.T[0:16, :]

    xbuf_ref[8:8 + ts, :] = proj_ref[:, 0:CONV_CH]
    conv = convw_ref[0:1, :] * xbuf_ref[5:5 + ts, :]
    for k in range(1, CONV_K):
        conv = conv + convw_ref[k:k + 1, :] * xbuf_ref[5 + k:5 + k + ts, :]
    xbuf_ref[0:8, :] = xbuf_ref[ts:ts + 8, :]
    qkv_ref[...] = _silu(conv)
    for hd in range(HEADS_AB):
        sl = slice(hd * HEAD_DIM_AB, (hd + 1) * HEAD_DIM_AB)
        q = qkv_ref[:, sl]
        qkv_ref[:, sl] = q * lax.rsqrt(jnp.sum(q * q, -1, keepdims=True) + EPS) * (HEAD_DIM_AB ** -0.5)
        sl = slice(QK_AB + hd * HEAD_DIM_AB, QK_AB + (hd + 1) * HEAD_DIM_AB)
        k = qkv_ref[:, sl]
        qkv_ref[:, sl] = k * lax.rsqrt(jnp.sum(k * k, -1, keepdims=True) + EPS)

    norm_a = norma_ref[...]

    def prep_body(i, carry):
        lmats, rhs = [], []
        for j in range(PREP_CHUNKS):
            c = i * PREP_CHUNKS + j
            r0 = pl.multiple_of(c * CHUNK, CHUNK)
            rows = pl.ds(r0, CHUNK)
            gct = gcumt_ref[c]
            for hd in range(HEADS_AB):
                lo, hi = hd * HEAD_DIM_AB, (hd + 1) * HEAD_DIM_AB
                q = qkv_ref[rows, lo:hi]
                k = qkv_ref[rows, QK_AB + lo:QK_AB + hi]
                v = qkv_ref[rows, 2 * QK_AB + lo:2 * QK_AB + hi]
                beta = gate_ref[rows, hd:hd + 1]
                gc_col = gcum_ref[rows, 4 + hd:5 + hd]
                gc_row = gct[4 + hd:5 + hd, :]
                g_last = gcum_ref[pl.ds(r0 + CHUNK - 1, 1), 4 + hd:5 + hd]
                diff = gc_col - gc_row
                decay = jnp.where(causal, jnp.exp(jnp.where(causal, diff, 0.0)), 0.0)
                kb = k * beta
                egc = jnp.exp(gc_col)
                k_bf = k.astype(BF16)
                lmats.append(jnp.where(strict, _dot_nt(kb, k_bf) * decay, 0.0))
                rhs.append((c, hd, rows, (v * beta).astype(BF16), (kb * egc).astype(BF16)))
                wq_ref[c, hd, CHUNK:2 * CHUNK, :] = (q * egc).astype(BF16)
                attn_ref[c, hd] = (_dot_nt(q, k_bf) * decay).astype(BF16)
                kdect_ref[c, hd] = (k * jnp.exp(g_last - gc_col)).T.astype(BF16)
                qb = proj_ref[rows, 4 * QK_AB + lo:4 * QK_AB + hi] * (HEAD_DIM_AB ** -0.5)
                kbb = proj_ref[rows, 5 * QK_AB + lo:5 * QK_AB + hi]
                bc_col = gcum_ref[rows, 12 + hd:13 + hd]
                dmat = jnp.where(causal, bc_col - gct[12 + hd:13 + hd, :] + gt_ref[c][8 + hd:9 + hd, :],
                                 -jnp.inf)
                dmat_ref[c, hd] = dmat
                mintra_ref[c, hd] = jnp.max(dmat, -1, keepdims=True)
                qk_ref[c, hd] = _dot_nt(qb, kbb)
                kbt_ref[c, hd] = kbb.T
        lmat = jnp.stack(lmats)
        l_split = _split(lmat)
        tinv = eye - lmat
        pw = _dot3(l_split, l_split)
        for _ in range(4):
            pw_split = _split(pw)
            tinv = tinv + _dot3(_split(tinv), pw_split)
            pw = _dot3(pw_split, pw_split)
        tinv = (tinv + _dot3(_split(tinv), _split(pw))).astype(BF16)
        for n, (c, hd, rows, vb, kbe) in enumerate(rhs):
            lo, hi = hd * HEAD_DIM_AB, (hd + 1) * HEAD_DIM_AB
            u_ref[rows, lo:hi] = jnp.dot(tinv[n], vb, preferred_element_type=F32)
            wq_ref[c, hd, 0:CHUNK, :] = jnp.dot(tinv[n], kbe, preferred_element_type=F32).astype(BF16)
        return carry

    lax.fori_loop(0, n_chunks // PREP_CHUNKS, prep_body, 0)

    def chunk_body(c, carry):
        r0 = pl.multiple_of(c * CHUNK, CHUNK)
        rows = pl.ds(r0, CHUNK)
        last = pl.ds(r0 + CHUNK - 1, 1)
        heads = range(HEADS_AB)
        col = lambda ref, off: jnp.stack([ref[rows, off + hd:off + hd + 1] for hd in heads])
        end = lambda off: jnp.stack([gcum_ref[last, off + hd:off + hd + 1] for hd in heads])
        wide = lambda ref, off: jnp.stack(
            [ref[rows, off + hd * HEAD_DIM_AB:off + (hd + 1) * HEAD_DIM_AB] for hd in heads])
        bdot = lambda x, y: lax.dot_general(x, y, (((2,), (1,)), ((0,), (0,))),
                                            preferred_element_type=F32)
        state = sa_ref[...]
        ws_qs = bdot(wq_ref[c], state.astype(BF16))
        qb = wide(proj_ref, 4 * QK_AB) * (HEAD_DIM_AB ** -0.5)
        qb_bf = qb.astype(BF16)
        c_st = cb_ref[...]
        q_c = bdot(qb_bf, c_st.astype(BF16))

        v_new = (wide(u_ref, 0) - ws_qs[:, 0:CHUNK]).astype(BF16)
        o = ws_qs[:, CHUNK:] + bdot(attn_ref[c], v_new)
        sa_ref[...] = state * jnp.exp(end(4)) + bdot(kdect_ref[c], v_new)

        vbb = wide(proj_ref, 6 * QK_AB).astype(BF16)
        kbb = wide(proj_ref, 5 * QK_AB)
        m_prev = mb_ref[:, 0:1, 0:1]
        n_st = nb_ref[:, 0:1, :]
        b_last = end(12)
        bc_col = col(gcum_ref, 12)
        a = bc_col + m_prev
        mt = jnp.maximum(a, mintra_ref[c])
        inter = jnp.exp(a - mt)
        pmat = qk_ref[c] * jnp.exp(dmat_ref[c] - mt)
        num = inter * q_c + bdot(pmat.astype(BF16), vbb)
        den = inter * jnp.sum(qb * n_st, -1, keepdims=True) + jnp.sum(pmat, -1, keepdims=True)
        hout = num / jnp.maximum(jnp.abs(den), jnp.exp(-mt))
        ws_col = b_last - bc_col + col(gate_ref, 8)
        ws_row = (b_last - jnp.stack([gcumt_ref[c][12 + hd:13 + hd, :] for hd in heads])
                  + jnp.stack([gt_ref[c][8 + hd:9 + hd, :] for hd in heads]))
        m_new = jnp.maximum(b_last + m_prev, jnp.max(ws_row, -1, keepdims=True))
        sc = jnp.exp(b_last + m_prev - m_new)
        kw_t = (kbt_ref[c] * jnp.exp(ws_row - m_new)).astype(BF16)
        cb_ref[...] = sc * c_st + bdot(kw_t, vbb)
        n_new = sc * n_st + jnp.sum(kbb * jnp.exp(ws_col - m_new), 1, keepdims=True)
        nb_ref[...] = jnp.broadcast_to(n_new, nb_ref.shape)
        mb_ref[...] = jnp.broadcast_to(m_new, mb_ref.shape)

        for hd in heads:
            lo, hi = hd * HEAD_DIM_AB, (hd + 1) * HEAD_DIM_AB
            z = proj_ref[rows, 3 * QK_AB + lo:3 * QK_AB + hi]
            out_ref[rows, lo:hi] = (_rms(o[hd], norm_a) * _silu(z)).astype(out_ref.dtype)
            og = proj_ref[rows, 7 * QK_AB + lo:7 * QK_AB + hi]
            out_ref[rows, QK_AB + lo:QK_AB + hi] = (
                _rms(hout[hd], normb_ref[hd:hd + 1, :]) * _sigmoid(og)).astype(out_ref.dtype)
        return carry

    lax.fori_loop(0, n_chunks, chunk_body, 0)


def _mixer_ab(h, g0, w_in, conv_w, a_log, dt_bias, norm_a, i_bias, f_bias, norm_b, batch, seq):
    tokens = batch * seq
    ts = min(SEQ_TILE, seq)
    tiles = seq // ts
    a_cols = 2 * QK_AB + QK_AB + 2 * HEADS_AB + QK_AB
    wa, wb = w_in[:, :a_cols], w_in[:, a_cols:]
    g_off = 3 * QK_AB
    z_off = g_off + 2 * HEADS_AB
    w_big = jnp.concatenate([wa[:, :g_off], wa[:, z_off:], wb[:, :g_off], wb[:, z_off:]], 1).astype(BF16)
    w_gate = jnp.concatenate([wa[:, g_off:z_off], wb[:, g_off:z_off],
                              jnp.zeros((D_MODEL, GATE_COLS - 4 * HEADS_AB), F32)], 1).astype(BF16)
    zeros4 = jnp.zeros((HEADS_AB,), F32)
    pad = jnp.zeros((GATE_COLS - 4 * HEADS_AB,), F32)
    bias_row = jnp.concatenate([zeros4, dt_bias, i_bias, f_bias, pad])
    alog_row = jnp.concatenate([zeros4, a_log, zeros4, zeros4, pad])
    gprm = jnp.concatenate([bias_row[None], alog_row[None], jnp.zeros((6, GATE_COLS), F32)], 0)

    const = lambda shape: pl.BlockSpec(shape, lambda b, t: (0,) * len(shape),
                                       pipeline_mode=pl.Buffered(1))
    n_chunks = ts // CHUNK
    return pl.pallas_call(
        _mixer_ab_kernel,
        grid=(batch, tiles),
        in_specs=[
            pl.BlockSpec((ts, D_MODEL), lambda b, t: (b * tiles + t, 0)),
            const((1, D_MODEL)),
            const((D_MODEL, BIG_COLS)),
            const((D_MODEL, GATE_COLS)),
            const((CONV_K, CONV_CH)),
            const((8, GATE_COLS)),
            const((1, HEAD_DIM_AB)),
            const((HEADS_AB, HEAD_DIM_AB)),
        ],
        out_specs=pl.BlockSpec((ts, 2 * QK_AB), lambda b, t: (b * tiles + t, 0)),
        out_shape=jax.ShapeDtypeStruct((tokens, 2 * QK_AB), BF16),
        scratch_shapes=[
            pltpu.VMEM((ts, BIG_COLS), F32),
            pltpu.VMEM((ts + 8, CONV_CH), F32),
            pltpu.VMEM((ts, CONV_CH), F32),
            pltpu.VMEM((ts, GATE_COLS), F32),
            pltpu.VMEM((ts, GATE_COLS), F32),
            pltpu.VMEM((n_chunks, 16, CHUNK), F32),
            pltpu.VMEM((n_chunks, 16, CHUNK), F32),
            pltpu.VMEM((ts, QK_AB), F32),
            pltpu.VMEM((n_chunks, HEADS_AB, 2 * CHUNK, HEAD_DIM_AB), BF16),
            pltpu.VMEM((n_chunks, HEADS_AB, CHUNK, CHUNK), BF16),
            pltpu.VMEM((n_chunks, HEADS_AB, HEAD_DIM_AB, CHUNK), BF16),
            pltpu.VMEM((n_chunks, HEADS_AB, CHUNK, CHUNK), F32),
            pltpu.VMEM((n_chunks, HEADS_AB, CHUNK, CHUNK), F32),
            pltpu.VMEM((n_chunks, HEADS_AB, CHUNK, 1), F32),
            pltpu.VMEM((n_chunks, HEADS_AB, HEAD_DIM_AB, CHUNK), F32),
            pltpu.VMEM((HEADS_AB, HEAD_DIM_AB, HEAD_DIM_AB), F32),
            pltpu.VMEM((HEADS_AB, HEAD_DIM_AB, HEAD_DIM_AB), F32),
            pltpu.VMEM((HEADS_AB, 8, HEAD_DIM_AB), F32),
            pltpu.VMEM((HEADS_AB, 8, 128), F32),
        ],
        compiler_params=pltpu.CompilerParams(
            dimension_semantics=("arbitrary", "arbitrary"),
            vmem_limit_bytes=VMEM_LIMIT_BYTES),
        name="mixer_ab",
    )(h, g0[None], w_big, w_gate, conv_w, gprm, norm_a[None], norm_b)


def _mixer_c_kernel(h_ref, pos_ref, invc_ref, sink_ref, g0_ref, wqt_ref, bq_ref, wkt_ref, bk_ref,
                    wvt_ref, bv_ref, out_ref, q_ref, klo_ref, khi_ref, vtlo_ref, vthi_ref):
    ts = h_ref.shape[0]
    n_blocks = ts // WINDOW
    half = C_HD // 2
    t = pl.program_id(1)

    @pl.when(t == 0)
    def _():
        vtlo_ref[...] = jnp.zeros_like(vtlo_ref)
        vthi_ref[...] = jnp.zeros_like(vthi_ref)
        klo_ref[0:WINDOW, :] = jnp.zeros((WINDOW, klo_ref.shape[1]), BF16)
        khi_ref[0:WINDOW, :] = jnp.zeros((WINDOW, khi_ref.shape[1]), BF16)

    hn = _rms(h_ref[...], g0_ref[...]).astype(BF16)

    ang_t = pos_ref[...].astype(F32) * invc_ref[...]
    cos_t = jnp.cos(ang_t)
    sin_t = jnp.sin(ang_t)

    def rope_t(x):
        x1, x2 = x[0:half], x[half:C_HD]
        return jnp.concatenate([x1 * cos_t - x2 * sin_t, x2 * cos_t + x1 * sin_t], 0)

    qt = _dot_nt(wqt_ref[...], hn) + bq_ref[...]
    for hd in range(C_HEADS):
        rows = slice(hd * C_HD, (hd + 1) * C_HD)
        q_ref[rows, :] = (rope_t(qt[rows]) * (C_HD ** -0.5)).astype(BF16)

    kt = _dot_nt(wkt_ref[...], hn) + bk_ref[...]
    kk = jnp.concatenate([rope_t(kt[j * C_HD:(j + 1) * C_HD]) for j in range(2 * C_KV_HEADS)], 0).T
    low = (lax.broadcasted_iota(jnp.int32, (1, 2 * C_KV), 1) % 128) < C_HD
    klo_ref[WINDOW:, :] = jnp.where(low, kk, 0.0).astype(BF16)
    khi_ref[WINDOW:, :] = jnp.where(low, 0.0, kk).astype(BF16)

    vt = (_dot_nt(wvt_ref[...], hn) + bv_ref[...]).astype(BF16)
    for g in range(C_KV_HEADS):
        vtlo_ref[2 * g * C_HD:(2 * g + 1) * C_HD, WINDOW:] = vt[g * C_HD:(g + 1) * C_HD]
        vthi_ref[(2 * g + 1) * C_HD:(2 * g + 2) * C_HD, WINDOW:] = vt[g * C_HD:(g + 1) * C_HD]

    bdot = lambda x, y: lax.dot_general(x, y, (((2,), (1,)), ((0,), (0,))), preferred_element_type=F32)
    groups = range(C_KV_HEADS)
    key_idx = lax.broadcasted_iota(jnp.int32, (WINDOW, 2 * WINDOW), 0)
    qry_idx = lax.broadcasted_iota(jnp.int32, (WINDOW, 2 * WINDOW), 1) % WINDOW
    from_prev = key_idx > qry_idx
    for blk in range(n_blocks):
        cols = slice(blk * WINDOW, (blk + 1) * WINDOW)
        win = slice(blk * WINDOW, (blk + 2) * WINDOW)
        kbd = jnp.stack([jnp.concatenate([klo_ref[win, g * 128:(g + 1) * 128],
                                          khi_ref[win, g * 128:(g + 1) * 128]], 0) for g in groups])
        qbd = jnp.stack([jnp.concatenate([q_ref[g * 256:g * 256 + 128, cols],
                                          q_ref[g * 256 + 128:(g + 1) * 256, cols]], 1) for g in groups])
        st = bdot(kbd, qbd)
        probs = []
        for odd in range(2):
            s_prev = st[:, odd * 2 * WINDOW:odd * 2 * WINDOW + WINDOW]
            s_cur = st[:, odd * 2 * WINDOW + WINDOW:(odd + 1) * 2 * WINDOW]
            sc = jnp.where(from_prev, s_prev, s_cur)
            if blk == 0:
                sc = jnp.where(from_prev & (key_idx < jnp.where(t > 0, -1, WINDOW)), -jnp.inf, sc)
            sink = sink_ref[odd]
            m = jnp.maximum(jnp.max(sc, 1, keepdims=True), sink)
            pr = jnp.exp(sc - m)
            den = jnp.sum(pr, 1, keepdims=True) + jnp.exp(sink - m)
            pr = pr / den
            probs += [jnp.where(from_prev, pr, 0.0).astype(BF16), jnp.where(from_prev, 0.0, pr).astype(BF16)]
        vbd = jnp.stack([jnp.concatenate([vtlo_ref[g * 128:(g + 1) * 128, win],
                                          vthi_ref[g * 128:(g + 1) * 128, win]], 1) for g in groups])
        ot = bdot(vbd, jnp.concatenate(probs, 1))
        for g in groups:
            o = ot[g].T
            out_ref[cols, g * 256:g * 256 + 128] = o[0:WINDOW].astype(out_ref.dtype)
            out_ref[cols, g * 256 + 128:(g + 1) * 256] = o[WINDOW:].astype(out_ref.dtype)

    klo_ref[0:WINDOW, :] = klo_ref[ts:ts + WINDOW, :]
    khi_ref[0:WINDOW, :] = khi_ref[ts:ts + WINDOW, :]
    vtlo_ref[:, 0:WINDOW] = vtlo_ref[:, ts:ts + WINDOW]
    vthi_ref[:, 0:WINDOW] = vthi_ref[:, ts:ts + WINDOW]


def _mixer_c(h, g0, positions, w_qkv, b_qkv, sinks, batch, seq):
    tokens = batch * seq
    ts = min(SEQ_TILE, seq)
    tiles = seq // ts
    half = C_HD // 2
    inv_col = (ROPE_THETA ** (-jnp.arange(half, dtype=F32) / half))[:, None]
    pos = positions.reshape(batch * tiles, 1, ts)
    wt, b_col = w_qkv.T.astype(BF16), b_qkv[:, None]
    w_q, b_q = wt[:C_Q], b_col[:C_Q]
    dup = lambda a: jnp.repeat(a.reshape(C_KV_HEADS, 1, C_HD, -1), 2, 1).reshape(2 * C_KV, -1)
    w_k, b_k = dup(wt[C_Q:C_Q + C_KV]), dup(b_col[C_Q:C_Q + C_KV])
    w_v, b_v = wt[C_Q + C_KV:], b_col[C_Q + C_KV:]
    sink_rows = jnp.repeat(sinks.reshape(C_KV_HEADS, 2, 2).transpose(2, 0, 1), WINDOW, -1)[:, :, None, :]
    const = lambda shape: pl.BlockSpec(shape, lambda b, t: (0,) * len(shape),
                                       pipeline_mode=pl.Buffered(1))
    return pl.pallas_call(
        _mixer_c_kernel,
        grid=(batch, tiles),
        in_specs=[
            pl.BlockSpec((ts, D_MODEL), lambda b, t: (b * tiles + t, 0)),
            pl.BlockSpec((None, 1, ts), lambda b, t: (b * tiles + t, 0, 0)),
            const((half, 1)),
            const((2, C_KV_HEADS, 1, 2 * WINDOW)),
            const((1, D_MODEL)),
            const((C_Q, D_MODEL)),
            const((C_Q, 1)),
            const((2 * C_KV, D_MODEL)),
            const((2 * C_KV, 1)),
            const((C_KV, D_MODEL)),
            const((C_KV, 1)),
        ],
        out_specs=pl.BlockSpec((ts, C_Q), lambda b, t: (b * tiles + t, 0)),
        out_shape=jax.ShapeDtypeStruct((tokens, C_Q), BF16),
        scratch_shapes=[
            pltpu.VMEM((C_Q, ts), BF16),
            pltpu.VMEM((WINDOW + ts, 2 * C_KV), BF16),
            pltpu.VMEM((WINDOW + ts, 2 * C_KV), BF16),
            pltpu.VMEM((2 * C_KV, WINDOW + ts), BF16),
            pltpu.VMEM((2 * C_KV, WINDOW + ts), BF16),
        ],
        compiler_params=pltpu.CompilerParams(
            dimension_semantics=("arbitrary", "arbitrary"),
            vmem_limit_bytes=VMEM_LIMIT_BYTES),
        name="mixer_c",
    )(h, pos, inv_col, sink_rows, g0[None], w_q, b_q, w_k, b_k, w_v, b_v)


def _post_kernel(mix_ref, h_ref, p_ref, wout_ref, bout_ref, gains_ref, wup_ref, wdown_ref,
                 wgate_ref, wple_ref, out_ref):
    mix = jnp.dot(mix_ref[...], wout_ref[...], preferred_element_type=F32) + bout_ref[...]
    h = h_ref[...] + _rms(mix, gains_ref[1:2, :])
    hn = _rms(h, gains_ref[2:3, :]).astype(BF16)
    slab = D_FF // FF_SPLIT
    ff = None
    for j in range(FF_SPLIT):
        up = jnp.dot(hn, wup_ref[:, j * slab:(j + 1) * slab], preferred_element_type=F32)
        act = jnp.square(jnp.maximum(up, 0.0)).astype(BF16)
        part = jnp.dot(act, wdown_ref[j * slab:(j + 1) * slab, :], preferred_element_type=F32)
        ff = part if ff is None else ff + part
    h = h + _rms(ff, gains_ref[3:4, :])
    gate = _sigmoid(jnp.dot(h.astype(BF16), wgate_ref[...], preferred_element_type=F32))
    ple = jnp.dot(p_ref[...].astype(BF16), wple_ref[...], preferred_element_type=F32)
    out_ref[...] = h + gate * ple


def _post(mixed, h, p_all, layer, w_out, b_out, gains, w_up, w_down, w_gate, w_ple):
    tokens = h.shape[0]
    tm = min(POST_TILE, tokens)
    const = lambda shape: pl.BlockSpec(shape, lambda i: (0,) * len(shape),
                                       pipeline_mode=pl.Buffered(1))
    return pl.pallas_call(
        _post_kernel,
        grid=(tokens // tm,),
        in_specs=[
            pl.BlockSpec((tm, D_MODEL), lambda i: (i, 0)),
            pl.BlockSpec((tm, D_MODEL), lambda i: (i, 0)),
            pl.BlockSpec((None, tm, PLE_DIM), lambda i: (layer, i, 0)),
            const((D_MODEL, D_MODEL)),
            const((1, D_MODEL)),
            const((4, D_MODEL)),
            const((D_MODEL, D_FF)),
            const((D_FF, D_MODEL)),
            const((D_MODEL, D_MODEL)),
            const((PLE_DIM, D_MODEL)),
        ],
        out_specs=pl.BlockSpec((tm, D_MODEL), lambda i: (i, 0)),
        out_shape=jax.ShapeDtypeStruct((tokens, D_MODEL), F32),
        compiler_params=pltpu.CompilerParams(
            dimension_semantics=("arbitrary",),
            vmem_limit_bytes=VMEM_LIMIT_BYTES),
        name="post",
    )(mixed, h, p_all, w_out.astype(BF16), b_out[None], gains, w_up.astype(BF16),
      w_down.astype(BF16), w_gate.astype(BF16), w_ple.astype(BF16))


def kernel(x, p, positions, norm_gains, w_in_ab, conv_a, a_log, dt_bias, norm_a, i_bias_b, f_bias_b, norm_b, w_out_ab, w_qkv_c, b_qkv_c, sinks_c, w_o_c, b_o_c, w_up, w_down, w_ple, w_ple_gate):
    batch, seq, _ = x.shape
    depth = norm_gains.shape[0]
    tokens = batch * seq
    h = x.reshape(tokens, D_MODEL)
    p_all = p.reshape(depth, tokens, PLE_DIM)
    for layer in range(depth):
        gains = norm_gains[layer]
        if layer % 2 == 0:
            e = layer // 2
            mixed = _mixer_ab(h, gains[0], w_in_ab[e], conv_a[e], a_log[e], dt_bias[e], norm_a[e],
                              i_bias_b[e], f_bias_b[e], norm_b[e], batch, seq)
            w_out, b_out = w_out_ab[e], jnp.zeros((D_MODEL,), F32)
        else:
            o = layer // 2
            mixed = _mixer_c(h, gains[0], positions, w_qkv_c[o], b_qkv_c[o], sinks_c[o], batch, seq)
            w_out, b_out = w_o_c[o], b_o_c[o]
        h = _post(mixed, h, p_all, layer, w_out, b_out, gains, w_up[layer], w_down[layer],
                  w_ple_gate[layer], w_ple[layer])
    return h.reshape(batch, seq, D_MODEL)
```

```python
import functools

import jax
import jax.numpy as jnp
from jax import lax
from jax.experimental import pallas as pl
from jax.experimental.pallas import tpu as pltpu

F32 = jnp.float32
BF16 = jnp.bfloat16
HIGHEST = lax.Precision.HIGHEST

EPS = 1e-6
D_MODEL = 1024
HEADS_AB = 4
HEAD_DIM_AB = 128
CONV_K = 4
CHUNK = 64
C_HEADS = 16
C_KV_HEADS = 4
C_GROUP = C_HEADS // C_KV_HEADS
C_HD = 64
WINDOW = 128
ROPE_THETA = 10000.0
D_FF = 4 * D_MODEL
PLE_DIM = 256
QK_AB = HEADS_AB * HEAD_DIM_AB
CONV_CH = 3 * QK_AB
BIG_COLS = 8 * QK_AB
GATE_COLS = 128
C_Q = C_HEADS * C_HD
C_KV = C_KV_HEADS * C_HD

SEQ_TILE = 512
POST_TILE = 512
PREP_CHUNKS = 8
FF_SPLIT = 4
VMEM_LIMIT_BYTES = 56 * 1024 * 1024


def _rms(x, g):
    return x * lax.rsqrt(jnp.mean(x * x, -1, keepdims=True) + EPS) * g


def _dot(a, b):
    return jnp.dot(a.astype(BF16), b.astype(BF16), preferred_element_type=F32)


def _dot_nt(a, b):
    return lax.dot_general(a.astype(BF16), b.astype(BF16), (((1,), (1,)), ((), ())),
                           preferred_element_type=F32)


def _dot_tn(a, b):
    return lax.dot_general(a.astype(BF16), b.astype(BF16), (((0,), (0,)), ((), ())),
                           preferred_element_type=F32)


def _dot_hi(a, b):
    return jnp.dot(a, b, preferred_element_type=F32, precision=HIGHEST)


def _split(a):
    hi = a.astype(BF16)
    hi_f = hi.astype(F32)
    lo_f = a - hi_f
    left = lax.broadcasted_iota(jnp.int32, a.shape, 2) < a.shape[2] // 2
    return hi, lo_f.astype(BF16), jnp.where(left, hi_f, lo_f).astype(BF16)


def _dot3(a, b):
    (ah, _, ahl), (bh, bl, _) = a, b
    d = lambda x, y: lax.dot_general(x, y, (((2,), (1,)), ((0,), (0,))), preferred_element_type=F32)
    return d(ahl, jnp.concatenate([bl, bh], 1)) + d(ah[:, :, 0:ah.shape[2] // 2], bh)


def _softplus(x):
    return jnp.maximum(x, 0.0) + jnp.log1p(jnp.exp(-jnp.abs(x)))


def _sigmoid(x):
    return 1.0 / (1.0 + jnp.exp(-x))


def _silu(x):
    return x * _sigmoid(x)


def _mixer_ab_kernel(h_ref, g0_ref, wbig_ref, wgate_ref, convw_ref, gprm_ref, norma_ref,
                     normb_ref, out_ref,
                     proj_ref, xbuf_ref, qkv_ref, gate_ref, gcum_ref, gt_ref, gcumt_ref, gcb_ref, bcb_ref,
                     u_ref, wq_ref, attn_ref, kdect_ref, eg_ref,
                     qbb_ref, pv0_ref, rs0_ref, mib_ref, kn0_ref, bl_ref, wm_ref,
                     sa_ref, cn_ref, mb_ref):
    ts = h_ref.shape[0]
    n_chunks = ts // CHUNK
    t = pl.program_id(1)

    @pl.when(t == 0)
    def _():
        xbuf_ref[0:8, :] = jnp.zeros((8, CONV_CH), F32)
        sa_ref[...] = jnp.zeros_like(sa_ref)
        cn_ref[...] = jnp.zeros_like(cn_ref)
        mb_ref[...] = jnp.zeros_like(mb_ref)

    hn = _rms(h_ref[...], g0_ref[...]).astype(BF16)
    gates_pre = jnp.dot(hn, wgate_ref[...], preferred_element_type=F32) + gprm_ref[0:1, :]

    lane = lax.broadcasted_iota(jnp.int32, (1, GATE_COLS), 1)
    neg_a = -jnp.exp(gprm_ref[1:2, :])
    gate = jnp.where(lane < 4, _sigmoid(gates_pre),
                     jnp.where(lane < 8, neg_a * _softplus(gates_pre),
                               jnp.where(lane < 12, gates_pre, -_softplus(-gates_pre))))
    gate_ref[...] = gate
    row = lax.broadcasted_iota(jnp.int32, (CHUNK, CHUNK), 0)
    col = lax.broadcasted_iota(jnp.int32, (CHUNK, CHUNK), 1)
    causal = row >= col
    tri = causal.astype(F32)
    row2 = lax.broadcasted_iota(jnp.int32, (CHUNK, 2 * CHUNK), 0)
    col2 = lax.broadcasted_iota(jnp.int32, (CHUNK, 2 * CHUNK), 1) % CHUNK
    causal2 = row2 >= col2
    strict2 = row2 > col2
    eye2 = (row2 == col2).astype(F32)
    twice = lambda x: jnp.concatenate([x, x], 1)
    for c in range(n_chunks):
        gch = gate_ref[c * CHUNK:(c + 1) * CHUNK, :]
        cum = _dot_hi(tri, gch)
        gcum_ref[c * CHUNK:(c + 1) * CHUNK, :] = cum
        gt_ref[c] = twice(gch.T[0:16, :])
        gcumt_ref[c] = twice(cum.T[0:16, :])
    for hd in range(HEADS_AB):
        gcb_ref[hd] = jnp.broadcast_to(gcum_ref[:, 4 + hd:5 + hd], (ts, HEAD_DIM_AB))
        bcb_ref[hd] = jnp.broadcast_to(gcum_ref[:, 12 + hd:13 + hd], (ts, HEAD_DIM_AB))

    proj_ref[...] = jnp.dot(hn, wbig_ref[...], preferred_element_type=F32)

    xbuf_ref[8:8 + ts, :] = proj_ref[:, 0:CONV_CH]
    conv = convw_ref[0:1, :] * xbuf_ref[5:5 + ts, :]
    for k in range(1, CONV_K):
        conv = conv + convw_ref[k:k + 1, :] * xbuf_ref[5 + k:5 + k + ts, :]
    xbuf_ref[0:8, :] = xbuf_ref[ts:ts + 8, :]
    qkv_ref[...] = _silu(conv)
    for hd in range(HEADS_AB):
        sl = slice(hd * HEAD_DIM_AB, (hd + 1) * HEAD_DIM_AB)
        q = qkv_ref[:, sl]
        qkv_ref[:, sl] = q * lax.rsqrt(jnp.sum(q * q, -1, keepdims=True) + EPS) * (HEAD_DIM_AB ** -0.5)
        sl = slice(QK_AB + hd * HEAD_DIM_AB, QK_AB + (hd + 1) * HEAD_DIM_AB)
        k = qkv_ref[:, sl]
        qkv_ref[:, sl] = k * lax.rsqrt(jnp.sum(k * k, -1, keepdims=True) + EPS)

    norm_a = norma_ref[...]

    def prep_body(i, carry):
        lmats, rhs = [], []
        for j in range(PREP_CHUNKS):
            c = i * PREP_CHUNKS + j
            r0 = pl.multiple_of(c * CHUNK, CHUNK)
            rows = pl.ds(r0, CHUNK)
            gct = gcumt_ref[c]
            for hd in range(HEADS_AB):
                lo, hi = hd * HEAD_DIM_AB, (hd + 1) * HEAD_DIM_AB
                q = qkv_ref[rows, lo:hi]
                k = qkv_ref[rows, QK_AB + lo:QK_AB + hi]
                v = qkv_ref[rows, 2 * QK_AB + lo:2 * QK_AB + hi]
                gtc = gt_ref[c]
                beta_row = gtc[hd:hd + 1, :]
                gc_row = gct[4 + hd:5 + hd, :]
                gc_all = gcb_ref[hd, rows, :]
                g_last = gc_all[CHUNK - 1:CHUNK, :]
                diff = gc_all - gc_row
                decay = jnp.where(causal2, jnp.exp(jnp.where(causal2, diff, 0.0)), 0.0)
                egc = jnp.exp(gc_all)
                k_bf = k.astype(BF16)
                kk = _dot_nt(k_bf, jnp.concatenate([k_bf, k_bf], 0))
                lmats.append(jnp.where(strict2, kk * decay, 0.0) * beta_row)
                rhs.append((c, hd, v.astype(BF16), (k * egc).astype(BF16)))
                wq_ref[c, hd, CHUNK:2 * CHUNK, :] = (q * egc).astype(BF16)
                attn_ref[c, hd] = (_dot_nt(q, k_bf) * (decay * beta_row)[:, 0:CHUNK]).astype(BF16)
                kdect_ref[c, hd] = ((k * jnp.exp(g_last - gc_all)).T * beta_row[:, 0:CHUNK]).astype(BF16)
                eg_ref[c, hd] = jnp.exp(g_last)
                qb = proj_ref[rows, 4 * QK_AB + lo:4 * QK_AB + hi] * (HEAD_DIM_AB ** -0.5)
                kbb = proj_ref[rows, 5 * QK_AB + lo:5 * QK_AB + hi]
                vbb = proj_ref[rows, 6 * QK_AB + lo:6 * QK_AB + hi].astype(BF16)
                bc_all = bcb_ref[hd, rows, :]
                bc_row = gct[12 + hd:13 + hd, 0:CHUNK]
                i_row = gtc[8 + hd:9 + hd, 0:CHUNK]
                b_last = bc_all[CHUNK - 1:CHUNK, :]
                dmat = jnp.where(causal, bc_all[:, 0:CHUNK] - bc_row + i_row, -jnp.inf)
                m_intra = jnp.max(dmat, -1, keepdims=True)
                p0 = _dot_nt(qb, kbb) * jnp.exp(dmat - m_intra)
                qbb_ref[c, hd] = qb.astype(BF16)
                pv0_ref[c, hd] = _dot(p0, vbb)
                rs0_ref[c, hd] = jnp.broadcast_to(jnp.sum(p0, -1, keepdims=True), (CHUNK, HEAD_DIM_AB))
                mib_ref[c, hd] = jnp.broadcast_to(m_intra, (CHUNK, HEAD_DIM_AB))
                ws_row = b_last[:, 0:CHUNK] - bc_row + i_row
                ws_max = jnp.max(ws_row, -1, keepdims=True)
                kw_t = kbb.T * jnp.exp(ws_row - ws_max)
                kn0_ref[c, hd, :, 0:HEAD_DIM_AB] = _dot(kw_t, vbb)
                kn0_ref[c, hd, :, HEAD_DIM_AB:] = jnp.broadcast_to(
                    jnp.sum(kw_t, -1, keepdims=True), (HEAD_DIM_AB, HEAD_DIM_AB))
                bl_ref[c, hd] = jnp.concatenate([b_last, b_last], 1)
                wm_ref[c, hd] = jnp.broadcast_to(ws_max, (1, 2 * HEAD_DIM_AB))
        lmat = jnp.stack(lmats)
        l_split = _split(lmat)
        tinv = eye2 - lmat
        pw = _dot3(l_split, l_split)
        for _ in range(4):
            pw_split = _split(pw)
            tinv = tinv + _dot3(_split(tinv), pw_split)
            pw = _dot3(pw_split, pw_split)
        tinv = (tinv + _dot3(_split(tinv), _split(pw)))[:, :, 0:CHUNK].astype(BF16)
        for n, (c, hd, v_bf, ke_bf) in enumerate(rhs):
            u_ref[c, hd] = jnp.dot(tinv[n], v_bf, preferred_element_type=F32)
            wq_ref[c, hd, 0:CHUNK, :] = jnp.dot(tinv[n], ke_bf, preferred_element_type=F32).astype(BF16)
        return carry

    lax.fori_loop(0, n_chunks // PREP_CHUNKS, prep_body, 0)

    def chunk_body(c, carry):
        r0 = pl.multiple_of(c * CHUNK, CHUNK)
        rows = pl.ds(r0, CHUNK)
        heads = range(HEADS_AB)
        bdot = lambda x, y: lax.dot_general(x, y, (((2,), (1,)), ((0,), (0,))),
                                            preferred_element_type=F32)
        state = sa_ref[...]
        ws_qs = bdot(wq_ref[c], state.astype(BF16))
        cn = cn_ref[...]
        q_cn = bdot(qbb_ref[c], cn.astype(BF16))

        v_new = (u_ref[c] - ws_qs[:, 0:CHUNK]).astype(BF16)
        o = ws_qs[:, CHUNK:] + bdot(attn_ref[c], v_new)
        sa_ref[...] = state * eg_ref[c] + bdot(kdect_ref[c], v_new)

        m_prev = mb_ref[:, 0:1, :]
        a = bcb_ref[:, rows, :] + m_prev[:, :, 0:HEAD_DIM_AB]
        m_intra = mib_ref[c]
        mt = jnp.maximum(a, m_intra)
        inter = jnp.exp(a - mt)
        intra = jnp.exp(m_intra - mt)
        num = inter * q_cn[:, :, 0:HEAD_DIM_AB] + intra * pv0_ref[c]
        den = inter * q_cn[:, :, HEAD_DIM_AB:] + intra * rs0_ref[c]
        hout = num / jnp.maximum(jnp.abs(den), jnp.exp(-mt))
        b_last, ws_max = bl_ref[c], wm_ref[c]
        m_new = jnp.maximum(b_last + m_prev, ws_max)
        cn_ref[...] = jnp.exp(b_last + m_prev - m_new) * cn + jnp.exp(ws_max - m_new) * kn0_ref[c]
        mb_ref[...] = jnp.broadcast_to(m_new, mb_ref.shape)

        for hd in heads:
            lo, hi = hd * HEAD_DIM_AB, (hd + 1) * HEAD_DIM_AB
            z = proj_ref[rows, 3 * QK_AB + lo:3 * QK_AB + hi]
            out_ref[rows, lo:hi] = (_rms(o[hd], norm_a) * _silu(z)).astype(out_ref.dtype)
            og = proj_ref[rows, 7 * QK_AB + lo:7 * QK_AB + hi]
            out_ref[rows, QK_AB + lo:QK_AB + hi] = (
                _rms(hout[hd], normb_ref[hd:hd + 1, :]) * _sigmoid(og)).astype(out_ref.dtype)
        return carry

    lax.fori_loop(0, n_chunks, chunk_body, 0)


def _mixer_ab(h, g0, w_in, conv_w, a_log, dt_bias, norm_a, i_bias, f_bias, norm_b, batch, seq):
    tokens = batch * seq
    ts = min(SEQ_TILE, seq)
    tiles = seq // ts
    a_cols = 2 * QK_AB + QK_AB + 2 * HEADS_AB + QK_AB
    wa, wb = w_in[:, :a_cols], w_in[:, a_cols:]
    g_off = 3 * QK_AB
    z_off = g_off + 2 * HEADS_AB
    w_big = jnp.concatenate([wa[:, :g_off], wa[:, z_off:], wb[:, :g_off], wb[:, z_off:]], 1).astype(BF16)
    w_gate = jnp.concatenate([wa[:, g_off:z_off], wb[:, g_off:z_off],
                              jnp.zeros((D_MODEL, GATE_COLS - 4 * HEADS_AB), F32)], 1).astype(BF16)
    zeros4 = jnp.zeros((HEADS_AB,), F32)
    pad = jnp.zeros((GATE_COLS - 4 * HEADS_AB,), F32)
    bias_row = jnp.concatenate([zeros4, dt_bias, i_bias, f_bias, pad])
    alog_row = jnp.concatenate([zeros4, a_log, zeros4, zeros4, pad])
    gprm = jnp.concatenate([bias_row[None], alog_row[None], jnp.zeros((6, GATE_COLS), F32)], 0)

    const = lambda shape: pl.BlockSpec(shape, lambda b, t: (0,) * len(shape),
                                       pipeline_mode=pl.Buffered(1))
    n_chunks = ts // CHUNK
    per_head = (n_chunks, HEADS_AB)
    return pl.pallas_call(
        _mixer_ab_kernel,
        grid=(batch, tiles),
        in_specs=[
            pl.BlockSpec((ts, D_MODEL), lambda b, t: (b * tiles + t, 0)),
            const((1, D_MODEL)),
            const((D_MODEL, BIG_COLS)),
            const((D_MODEL, GATE_COLS)),
            const((CONV_K, CONV_CH)),
            const((8, GATE_COLS)),
            const((1, HEAD_DIM_AB)),
            const((HEADS_AB, HEAD_DIM_AB)),
        ],
        out_specs=pl.BlockSpec((ts, 2 * QK_AB), lambda b, t: (b * tiles + t, 0)),
        out_shape=jax.ShapeDtypeStruct((tokens, 2 * QK_AB), BF16),
        scratch_shapes=[
            pltpu.VMEM((ts, BIG_COLS), F32),
            pltpu.VMEM((ts + 8, CONV_CH), F32),
            pltpu.VMEM((ts, CONV_CH), F32),
            pltpu.VMEM((ts, GATE_COLS), F32),
            pltpu.VMEM((ts, GATE_COLS), F32),
            pltpu.VMEM((n_chunks, 16, 2 * CHUNK), F32),
            pltpu.VMEM((n_chunks, 16, 2 * CHUNK), F32),
            pltpu.VMEM((HEADS_AB, ts, HEAD_DIM_AB), F32),
            pltpu.VMEM((HEADS_AB, ts, HEAD_DIM_AB), F32),
            pltpu.VMEM(per_head + (CHUNK, HEAD_DIM_AB), F32),
            pltpu.VMEM(per_head + (2 * CHUNK, HEAD_DIM_AB), BF16),
            pltpu.VMEM(per_head + (CHUNK, CHUNK), BF16),
            pltpu.VMEM(per_head + (HEAD_DIM_AB, CHUNK), BF16),
            pltpu.VMEM(per_head + (1, HEAD_DIM_AB), F32),
            pltpu.VMEM(per_head + (CHUNK, HEAD_DIM_AB), BF16),
            pltpu.VMEM(per_head + (CHUNK, HEAD_DIM_AB), F32),
            pltpu.VMEM(per_head + (CHUNK, HEAD_DIM_AB), F32),
            pltpu.VMEM(per_head + (CHUNK, HEAD_DIM_AB), F32),
            pltpu.VMEM(per_head + (HEAD_DIM_AB, 2 * HEAD_DIM_AB), F32),
            pltpu.VMEM(per_head + (1, 2 * HEAD_DIM_AB), F32),
            pltpu.VMEM(per_head + (1, 2 * HEAD_DIM_AB), F32),
            pltpu.VMEM((HEADS_AB, HEAD_DIM_AB, HEAD_DIM_AB), F32),
            pltpu.VMEM((HEADS_AB, HEAD_DIM_AB, 2 * HEAD_DIM_AB), F32),
            pltpu.VMEM((HEADS_AB, 8, 2 * HEAD_DIM_AB), F32),
        ],
        compiler_params=pltpu.CompilerParams(
            dimension_semantics=("arbitrary", "arbitrary"),
            vmem_limit_bytes=VMEM_LIMIT_BYTES),
        name="mixer_ab",
    )(h, g0[None], w_big, w_gate, conv_w, gprm, norm_a[None], norm_b)


def _mixer_c_kernel(h_ref, pos_ref, invc_ref, sink_ref, g0_ref, wqt_ref, bq_ref, wkt_ref, bk_ref,
                    wvt_ref, bv_ref, out_ref, q_ref, klo_ref, khi_ref, vtlo_ref, vthi_ref):
    ts = h_ref.shape[0]
    n_blocks = ts // WINDOW
    half = C_HD // 2
    t = pl.program_id(1)

    @pl.when(t == 0)
    def _():
        vtlo_ref[...] = jnp.zeros_like(vtlo_ref)
        vthi_ref[...] = jnp.zeros_like(vthi_ref)
        klo_ref[0:WINDOW, :] = jnp.zeros((WINDOW, klo_ref.shape[1]), BF16)
        khi_ref[0:WINDOW, :] = jnp.zeros((WINDOW, khi_ref.shape[1]), BF16)

    hn = _rms(h_ref[...], g0_ref[...]).astype(BF16)

    ang_t = pos_ref[...].astype(F32) * invc_ref[...]
    cos_t = jnp.cos(ang_t)
    sin_t = jnp.sin(ang_t)

    def rope_t(x):
        x1, x2 = x[0:half], x[half:C_HD]
        return jnp.concatenate([x1 * cos_t - x2 * sin_t, x2 * cos_t + x1 * sin_t], 0)

    qt = _dot_nt(wqt_ref[...], hn) + bq_ref[...]
    for hd in range(C_HEADS):
        rows = slice(hd * C_HD, (hd + 1) * C_HD)
        q_ref[rows, :] = (rope_t(qt[rows]) * (C_HD ** -0.5)).astype(BF16)

    kt = _dot_nt(wkt_ref[...], hn) + bk_ref[...]
    kk = jnp.concatenate([rope_t(kt[j * C_HD:(j + 1) * C_HD]) for j in range(2 * C_KV_HEADS)], 0).T
    low = (lax.broadcasted_iota(jnp.int32, (1, 2 * C_KV), 1) % 128) < C_HD
    klo_ref[WINDOW:, :] = jnp.where(low, kk, 0.0).astype(BF16)
    khi_ref[WINDOW:, :] = jnp.where(low, 0.0, kk).astype(BF16)

    vt = (_dot_nt(wvt_ref[...], hn) + bv_ref[...]).astype(BF16)
    for g in range(C_KV_HEADS):
        vtlo_ref[2 * g * C_HD:(2 * g + 1) * C_HD, WINDOW:] = vt[g * C_HD:(g + 1) * C_HD]
        vthi_ref[(2 * g + 1) * C_HD:(2 * g + 2) * C_HD, WINDOW:] = vt[g * C_HD:(g + 1) * C_HD]

    bdot = lambda x, y: lax.dot_general(x, y, (((2,), (1,)), ((0,), (0,))), preferred_element_type=F32)
    groups = range(C_KV_HEADS)
    key_idx = lax.broadcasted_iota(jnp.int32, (WINDOW, 2 * WINDOW), 0)
    qry_idx = lax.broadcasted_iota(jnp.int32, (WINDOW, 2 * WINDOW), 1) % WINDOW
    from_prev = key_idx > qry_idx
    for blk in range(n_blocks):
        cols = slice(blk * WINDOW, (blk + 1) * WINDOW)
        win = slice(blk * WINDOW, (blk + 2) * WINDOW)
        kbd = jnp.stack([jnp.concatenate([klo_ref[win, g * 128:(g + 1) * 128],
                                          khi_ref[win, g * 128:(g + 1) * 128]], 0) for g in groups])
        qbd = jnp.stack([jnp.concatenate([q_ref[g * 256:g * 256 + 128, cols],
                                          q_ref[g * 256 + 128:(g + 1) * 256, cols]], 1) for g in groups])
        st = bdot(kbd, qbd)
        probs = []
        for odd in range(2):
            s_prev = st[:, odd * 2 * WINDOW:odd * 2 * WINDOW + WINDOW]
            s_cur = st[:, odd * 2 * WINDOW + WINDOW:(odd + 1) * 2 * WINDOW]
            sc = jnp.where(from_prev, s_prev, s_cur)
            if blk == 0:
                sc = jnp.where(from_prev & (key_idx < jnp.where(t > 0, -1, WINDOW)), -jnp.inf, sc)
            sink = sink_ref[odd]
            m = jnp.maximum(jnp.max(sc, 1, keepdims=True), sink)
            pr = jnp.exp(sc - m)
            den = jnp.sum(pr, 1, keepdims=True) + jnp.exp(sink - m)
            pr = pr / den
            probs += [jnp.where(from_prev, pr, 0.0).astype(BF16), jnp.where(from_prev, 0.0, pr).astype(BF16)]
        vbd = jnp.stack([jnp.concatenate([vtlo_ref[g * 128:(g + 1) * 128, win],
                                          vthi_ref[g * 128:(g + 1) * 128, win]], 1) for g in groups])
        ot = bdot(vbd, jnp.concatenate(probs, 1))
        for g in groups:
            o = ot[g].T
            out_ref[cols, g * 256:g * 256 + 128] = o[0:WINDOW].astype(out_ref.dtype)
            out_ref[cols, g * 256 + 128:(g + 1) * 256] = o[WINDOW:].astype(out_ref.dtype)

    klo_ref[0:WINDOW, :] = klo_ref[ts:ts + WINDOW, :]
    khi_ref[0:WINDOW, :] = khi_ref[ts:ts + WINDOW, :]
    vtlo_ref[:, 0:WINDOW] = vtlo_ref[:, ts:ts + WINDOW]
    vthi_ref[:, 0:WINDOW] = vthi_ref[:, ts:ts + WINDOW]


def _mixer_c(h, g0, positions, w_qkv, b_qkv, sinks, batch, seq):
    tokens = batch * seq
    ts = min(SEQ_TILE, seq)
    tiles = seq // ts
    half = C_HD // 2
    inv_col = (ROPE_THETA ** (-jnp.arange(half, dtype=F32) / half))[:, None]
    pos = positions.reshape(batch * tiles, 1, ts)
    wt, b_col = w_qkv.T.astype(BF16), b_qkv[:, None]
    w_q, b_q = wt[:C_Q], b_col[:C_Q]
    dup = lambda a: jnp.repeat(a.reshape(C_KV_HEADS, 1, C_HD, -1), 2, 1).reshape(2 * C_KV, -1)
    w_k, b_k = dup(wt[C_Q:C_Q + C_KV]), dup(b_col[C_Q:C_Q + C_KV])
    w_v, b_v = wt[C_Q + C_KV:], b_col[C_Q + C_KV:]
    sink_rows = jnp.repeat(sinks.reshape(C_KV_HEADS, 2, 2).transpose(2, 0, 1), WINDOW, -1)[:, :, None, :]
    const = lambda shape: pl.BlockSpec(shape, lambda b, t: (0,) * len(shape),
                                       pipeline_mode=pl.Buffered(1))
    return pl.pallas_call(
        _mixer_c_kernel,
        grid=(batch, tiles),
        in_specs=[
            pl.BlockSpec((ts, D_MODEL), lambda b, t: (b * tiles + t, 0)),
            pl.BlockSpec((None, 1, ts), lambda b, t: (b * tiles + t, 0, 0)),
            const((half, 1)),
            const((2, C_KV_HEADS, 1, 2 * WINDOW)),
            const((1, D_MODEL)),
            const((C_Q, D_MODEL)),
            const((C_Q, 1)),
            const((2 * C_KV, D_MODEL)),
            const((2 * C_KV, 1)),
            const((C_KV, D_MODEL)),
            const((C_KV, 1)),
        ],
        out_specs=pl.BlockSpec((ts, C_Q), lambda b, t: (b * tiles + t, 0)),
        out_shape=jax.ShapeDtypeStruct((tokens, C_Q), BF16),
        scratch_shapes=[
            pltpu.VMEM((C_Q, ts), BF16),
            pltpu.VMEM((WINDOW + ts, 2 * C_KV), BF16),
            pltpu.VMEM((WINDOW + ts, 2 * C_KV), BF16),
            pltpu.VMEM((2 * C_KV, WINDOW + ts), BF16),
            pltpu.VMEM((2 * C_KV, WINDOW + ts), BF16),
        ],
        compiler_params=pltpu.CompilerParams(
            dimension_semantics=("arbitrary", "arbitrary"),
            vmem_limit_bytes=VMEM_LIMIT_BYTES),
        name="mixer_c",
    )(h, pos, inv_col, sink_rows, g0[None], w_q, b_q, w_k, b_k, w_v, b_v)


def _post_kernel(mix_ref, h_ref, p_ref, wout_ref, bout_ref, gains_ref, wup_ref, wdown_ref,
                 wgate_ref, wple_ref, out_ref):
    mix = jnp.dot(mix_ref[...], wout_ref[...], preferred_element_type=F32) + bout_ref[...]
    h = h_ref[...] + _rms(mix, gains_ref[1:2, :])
    hn = _rms(h, gains_ref[2:3, :]).astype(BF16)
    slab = D_FF // FF_SPLIT
    ff = None
    for j in range(FF_SPLIT):
        up = jnp.dot(hn, wup_ref[:, j * slab:(j + 1) * slab], preferred_element_type=F32)
        act = jnp.square(jnp.maximum(up, 0.0)).astype(BF16)
        part = jnp.dot(act, wdown_ref[j * slab:(j + 1) * slab, :], preferred_element_type=F32)
        ff = part if ff is None else ff + part
    h = h + _rms(ff, gains_ref[3:4, :])
    gate = _sigmoid(jnp.dot(h.astype(BF16), wgate_ref[...], preferred_element_type=F32))
    ple = jnp.dot(p_ref[...].astype(BF16), wple_ref[...], preferred_element_type=F32)
    out_ref[...] = h + gate * ple


def _post(mixed, h, p_all, layer, w_out, b_out, gains, w_up, w_down, w_gate, w_ple):
    tokens = h.shape[0]
    tm = min(POST_TILE, tokens)
    const = lambda shape: pl.BlockSpec(shape, lambda i: (0,) * len(shape),
                                       pipeline_mode=pl.Buffered(1))
    return pl.pallas_call(
        _post_kernel,
        grid=(tokens // tm,),
        in_specs=[
            pl.BlockSpec((tm, D_MODEL), lambda i: (i, 0)),
            pl.BlockSpec((tm, D_MODEL), lambda i: (i, 0)),
            pl.BlockSpec((None, tm, PLE_DIM), lambda i: (layer, i, 0)),
            const((D_MODEL, D_MODEL)),
            const((1, D_MODEL)),
            const((4, D_MODEL)),
            const((D_MODEL, D_FF)),
            const((D_FF, D_MODEL)),
            const((D_MODEL, D_MODEL)),
            const((PLE_DIM, D_MODEL)),
        ],
        out_specs=pl.BlockSpec((tm, D_MODEL), lambda i: (i, 0)),
        out_shape=jax.ShapeDtypeStruct((tokens, D_MODEL), F32),
        compiler_params=pltpu.CompilerParams(
            dimension_semantics=("arbitrary",),
            vmem_limit_bytes=VMEM_LIMIT_BYTES),
        name="post",
    )(mixed, h, p_all, w_out.astype(BF16), b_out[None], gains, w_up.astype(BF16),
      w_down.astype(BF16), w_gate.astype(BF16), w_ple.astype(BF16))


def kernel(x, p, positions, norm_gains, w_in_ab, conv_a, a_log, dt_bias, norm_a, i_bias_b, f_bias_b, norm_b, w_out_ab, w_qkv_c, b_qkv_c, sinks_c, w_o_c, b_o_c, w_up, w_down, w_ple, w_ple_gate):
    batch, seq, _ = x.shape
    depth = norm_gains.shape[0]
    tokens = batch * seq
    h = x.reshape(tokens, D_MODEL)
    p_all = p.reshape(depth, tokens, PLE_DIM)
    for layer in range(depth):
        gains = norm_gains[layer]
        if layer % 2 == 0:
            e = layer // 2
            mixed = _mixer_ab(h, gains[0], w_in_ab[e], conv_a[e], a_log[e], dt_bias[e], norm_a[e],
                              i_bias_b[e], f_bias_b[e], norm_b[e], batch, seq)
            w_out, b_out = w_out_ab[e], jnp.zeros((D_MODEL,), F32)
        else:
            o = layer // 2
            mixed = _mixer_c(h, gains[0], positions, w_qkv_c[o], b_qkv_c[o], sinks_c[o], batch, seq)
            w_out, b_out = w_o_c[o], b_o_c[o]
        h = _post(mixed, h, p_all, layer, w_out, b_out, gains, w_up[layer], w_down[layer],
                  w_ple_gate[layer], w_ple[layer])
    return h.reshape(batch, seq, D_MODEL)
```

```python
import functools

import jax
import jax.numpy as jnp
from jax import lax
from jax.experimental import pallas as pl
from jax.experimental.pallas import tpu as pltpu

F32 = jnp.float32
BF16 = jnp.bfloat16
HIGHEST = lax.Precision.HIGHEST

EPS = 1e-6
D_MODEL = 1024
HEADS_AB = 4
HEAD_DIM_AB = 128
CONV_K = 4
CHUNK = 64
C_HEADS = 16
C_KV_HEADS = 4
C_GROUP = C_HEADS // C_KV_HEADS
C_HD = 64
WINDOW = 128
ROPE_THETA = 10000.0
D_FF = 4 * D_MODEL
PLE_DIM = 256
QK_AB = HEADS_AB * HEAD_DIM_AB
CONV_CH = 3 * QK_AB
BIG_COLS = 8 * QK_AB
GATE_COLS = 128
C_Q = C_HEADS * C_HD
C_KV = C_KV_HEADS * C_HD

SEQ_TILE = 512
POST_TILE = 512
PREP_CHUNKS = 8
SCORES_AHEAD = 4
SEQ_UNROLL = 4
FF_SPLIT = 4
VMEM_LIMIT_BYTES = 56 * 1024 * 1024


def _rms(x, g):
    return x * lax.rsqrt(jnp.mean(x * x, -1, keepdims=True) + EPS) * g


def _dot(a, b):
    return jnp.dot(a.astype(BF16), b.astype(BF16), preferred_element_type=F32)


def _dot_nt(a, b):
    return lax.dot_general(a.astype(BF16), b.astype(BF16), (((1,), (1,)), ((), ())),
                           preferred_element_type=F32)


def _dot_tn(a, b):
    return lax.dot_general(a.astype(BF16), b.astype(BF16), (((0,), (0,)), ((), ())),
                           preferred_element_type=F32)


def _dot_hi(a, b):
    return jnp.dot(a, b, preferred_element_type=F32, precision=HIGHEST)


def _split(a):
    hi = a.astype(BF16)
    hi_f = hi.astype(F32)
    lo_f = a - hi_f
    left = lax.broadcasted_iota(jnp.int32, a.shape, 2) < a.shape[2] // 2
    return hi, lo_f.astype(BF16), jnp.where(left, hi_f, lo_f).astype(BF16)


def _dot3(a, b):
    (ah, _, ahl), (bh, bl, _) = a, b
    d = lambda x, y: lax.dot_general(x, y, (((2,), (1,)), ((0,), (0,))), preferred_element_type=F32)
    return d(ahl, jnp.concatenate([bl, bh], 1)) + d(ah[:, :, 0:ah.shape[2] // 2], bh)


def _softplus(x):
    return jnp.maximum(x, 0.0) + jnp.log1p(jnp.exp(-jnp.abs(x)))


def _sigmoid(x):
    return 1.0 / (1.0 + jnp.exp(-x))


def _silu(x):
    return x * _sigmoid(x)


def _mixer_ab_kernel(h_ref, g0_ref, wbig_ref, wgate_ref, convw_ref, gprm_ref, norma_ref,
                     normb_ref, out_ref,
                     proj_ref, xbuf_ref, qkv_ref, gate_ref, gcum_ref, gt_ref, gcumt_ref, gcb_ref, bcb_ref,
                     u_ref, wq_ref, attn_ref, kdect_ref, eg_ref,
                     qbb_ref, pv0_ref, rs0_ref, mib_ref, kn0_ref, bl_ref, wm_ref,
                     sa_ref, cn_ref, mb_ref):
    ts = h_ref.shape[0]
    n_chunks = ts // CHUNK
    t = pl.program_id(1)

    @pl.when(t == 0)
    def _():
        xbuf_ref[0:8, :] = jnp.zeros((8, CONV_CH), F32)
        sa_ref[...] = jnp.zeros_like(sa_ref)
        cn_ref[...] = jnp.zeros_like(cn_ref)
        mb_ref[...] = jnp.zeros_like(mb_ref)

    hn = _rms(h_ref[...], g0_ref[...]).astype(BF16)
    gates_pre = jnp.dot(hn, wgate_ref[...], preferred_element_type=F32) + gprm_ref[0:1, :]

    lane = lax.broadcasted_iota(jnp.int32, (1, GATE_COLS), 1)
    neg_a = -jnp.exp(gprm_ref[1:2, :])
    gate = jnp.where(lane < 4, _sigmoid(gates_pre),
                     jnp.where(lane < 8, neg_a * _softplus(gates_pre),
                               jnp.where(lane < 12, gates_pre, -_softplus(-gates_pre))))
    gate_ref[...] = gate
    row = lax.broadcasted_iota(jnp.int32, (CHUNK, CHUNK), 0)
    col = lax.broadcasted_iota(jnp.int32, (CHUNK, CHUNK), 1)
    causal = row >= col
    tri = causal.astype(F32)
    row2 = lax.broadcasted_iota(jnp.int32, (CHUNK, 2 * CHUNK), 0)
    col2 = lax.broadcasted_iota(jnp.int32, (CHUNK, 2 * CHUNK), 1) % CHUNK
    causal2 = row2 >= col2
    strict2 = row2 > col2
    eye2 = (row2 == col2).astype(F32)
    twice = lambda x: jnp.concatenate([x, x], 1)
    for c in range(n_chunks):
        gch = gate_ref[c * CHUNK:(c + 1) * CHUNK, :]
        cum = _dot_hi(tri, gch)
        gcum_ref[c * CHUNK:(c + 1) * CHUNK, :] = cum
        gt_ref[c] = twice(gch.T[0:16, :])
        gcumt_ref[c] = twice(cum.T[0:16, :])
    for hd in range(HEADS_AB):
        gcb_ref[hd] = jnp.broadcast_to(gcum_ref[:, 4 + hd:5 + hd], (ts, HEAD_DIM_AB))
        bcb_ref[hd] = jnp.broadcast_to(gcum_ref[:, 12 + hd:13 + hd], (ts, HEAD_DIM_AB))

    proj_ref[...] = jnp.dot(hn, wbig_ref[...], preferred_element_type=F32)
    proj_ref[:, 3 * QK_AB:4 * QK_AB] = _silu(proj_ref[:, 3 * QK_AB:4 * QK_AB])
    proj_ref[:, 7 * QK_AB:8 * QK_AB] = _sigmoid(proj_ref[:, 7 * QK_AB:8 * QK_AB])

    xbuf_ref[8:8 + ts, :] = proj_ref[:, 0:CONV_CH]
    xfull = xbuf_ref[...]
    conv = convw_ref[0:1, :] * xfull
    for k in range(1, CONV_K):
        conv = convw_ref[k:k + 1, :] * xfull + pltpu.roll(conv, 1, 0)
    xbuf_ref[0:8, :] = xbuf_ref[ts:ts + 8, :]
    qkv_ref[...] = _silu(conv[8:, :])
    for hd in range(HEADS_AB):
        sl = slice(hd * HEAD_DIM_AB, (hd + 1) * HEAD_DIM_AB)
        q = qkv_ref[:, sl]
        qkv_ref[:, sl] = q * lax.rsqrt(jnp.sum(q * q, -1, keepdims=True) + EPS) * (HEAD_DIM_AB ** -0.5)
        sl = slice(QK_AB + hd * HEAD_DIM_AB, QK_AB + (hd + 1) * HEAD_DIM_AB)
        k = qkv_ref[:, sl]
        qkv_ref[:, sl] = k * lax.rsqrt(jnp.sum(k * k, -1, keepdims=True) + EPS)

    norm_a = norma_ref[...]

    def prep_body(i, carry):
        lmats, rhs = [], []
        for j in range(PREP_CHUNKS):
            c = i * PREP_CHUNKS + j
            r0 = pl.multiple_of(c * CHUNK, CHUNK)
            rows = pl.ds(r0, CHUNK)
            gct = gcumt_ref[c]
            for hd in range(HEADS_AB):
                lo, hi = hd * HEAD_DIM_AB, (hd + 1) * HEAD_DIM_AB
                q = qkv_ref[rows, lo:hi]
                k = qkv_ref[rows, QK_AB + lo:QK_AB + hi]
                v = qkv_ref[rows, 2 * QK_AB + lo:2 * QK_AB + hi]
                gtc = gt_ref[c]
                beta_row = gtc[hd:hd + 1, :]
                gc_row = gct[4 + hd:5 + hd, :]
                gc_all = gcb_ref[hd, rows, :]
                g_last = gc_all[CHUNK - 1:CHUNK, :]
                diff = gc_all - gc_row
                decay = jnp.where(causal2, jnp.exp(jnp.where(causal2, diff, 0.0)), 0.0)
                egc = jnp.exp(gc_all)
                k_bf = k.astype(BF16)
                kk = _dot_nt(k_bf, jnp.concatenate([k_bf, k_bf], 0))
                lmats.append(jnp.where(strict2, kk * decay, 0.0) * beta_row)
                rhs.append((c, hd, v.astype(BF16), (k * egc).astype(BF16)))
                wq_ref[c, hd, CHUNK:2 * CHUNK, :] = (q * egc).astype(BF16)
                attn_ref[c, hd] = (_dot_nt(q, k_bf) * (decay * beta_row)[:, 0:CHUNK]).astype(BF16)
                kdect_ref[c, hd] = ((k * jnp.exp(g_last - gc_all)).T * beta_row[:, 0:CHUNK]).astype(BF16)
                eg_ref[c, hd] = jnp.exp(g_last)
                qb = proj_ref[rows, 4 * QK_AB + lo:4 * QK_AB + hi] * (HEAD_DIM_AB ** -0.5)
                kbb = proj_ref[rows, 5 * QK_AB + lo:5 * QK_AB + hi]
                vbb = proj_ref[rows, 6 * QK_AB + lo:6 * QK_AB + hi].astype(BF16)
                bc_all = bcb_ref[hd, rows, :]
                bc_row = gct[12 + hd:13 + hd, 0:CHUNK]
                i_row = gtc[8 + hd:9 + hd, 0:CHUNK]
                b_last = bc_all[CHUNK - 1:CHUNK, :]
                dmat = jnp.where(causal, bc_all[:, 0:CHUNK] - bc_row + i_row, -jnp.inf)
                m_intra = jnp.max(dmat, -1, keepdims=True)
                p0 = _dot_nt(qb, kbb) * jnp.exp(dmat - m_intra)
                qbb_ref[c, hd] = qb.astype(BF16)
                pv0_ref[c, hd] = _dot(p0, vbb)
                rs0_ref[c, hd] = jnp.broadcast_to(jnp.sum(p0, -1, keepdims=True), (CHUNK, HEAD_DIM_AB))
                mib_ref[c, hd] = jnp.broadcast_to(m_intra, (CHUNK, HEAD_DIM_AB))
                ws_row = b_last[:, 0:CHUNK] - bc_row + i_row
                ws_max = jnp.max(ws_row, -1, keepdims=True)
                kw_t = kbb.T * jnp.exp(ws_row - ws_max)
                kn0_ref[c, hd, :, 0:HEAD_DIM_AB] = _dot(kw_t, vbb)
                kn0_ref[c, hd, :, HEAD_DIM_AB:] = jnp.broadcast_to(
                    jnp.sum(kw_t, -1, keepdims=True), (HEAD_DIM_AB, HEAD_DIM_AB))
                bl_ref[c, hd] = jnp.concatenate([b_last, b_last], 1)
                wm_ref[c, hd] = jnp.broadcast_to(ws_max, (1, 2 * HEAD_DIM_AB))
        lmat = jnp.stack(lmats)
        l_split = _split(lmat)
        tinv = eye2 - lmat
        pw = _dot3(l_split, l_split)
        for _ in range(4):
            pw_split = _split(pw)
            tinv = tinv + _dot3(_split(tinv), pw_split)
            pw = _dot3(pw_split, pw_split)
        tinv = (tinv + _dot3(_split(tinv), _split(pw)))[:, :, 0:CHUNK].astype(BF16)
        for n, (c, hd, v_bf, ke_bf) in enumerate(rhs):
            u_ref[c, hd] = jnp.dot(tinv[n], v_bf, preferred_element_type=F32)
            wq_ref[c, hd, 0:CHUNK, :] = jnp.dot(tinv[n], ke_bf, preferred_element_type=F32).astype(BF16)
        return carry

    lax.fori_loop(0, n_chunks // PREP_CHUNKS, prep_body, 0)

    def chunk_body(c, carry):
        r0 = pl.multiple_of(c * CHUNK, CHUNK)
        rows = pl.ds(r0, CHUNK)
        heads = range(HEADS_AB)
        bdot = lambda x, y: lax.dot_general(x, y, (((2,), (1,)), ((0,), (0,))),
                                            preferred_element_type=F32)
        state = sa_ref[...]
        ws_qs = bdot(wq_ref[c], state.astype(BF16))
        cn = cn_ref[...]
        q_cn = bdot(qbb_ref[c], cn.astype(BF16))

        v_new = (u_ref[c] - ws_qs[:, 0:CHUNK]).astype(BF16)
        o = ws_qs[:, CHUNK:] + bdot(attn_ref[c], v_new)
        sa_ref[...] = state * eg_ref[c] + bdot(kdect_ref[c], v_new)

        m_prev = mb_ref[:, 0:1, :]
        a = bcb_ref[:, rows, :] + m_prev[:, :, 0:HEAD_DIM_AB]
        m_intra = mib_ref[c]
        mt = jnp.maximum(a, m_intra)
        inter = jnp.exp(a - mt)
        intra = jnp.exp(m_intra - mt)
        num = inter * q_cn[:, :, 0:HEAD_DIM_AB] + intra * pv0_ref[c]
        den = inter * q_cn[:, :, HEAD_DIM_AB:] + intra * rs0_ref[c]
        hout = num / jnp.maximum(jnp.abs(den), jnp.exp(-mt))
        b_last, ws_max = bl_ref[c], wm_ref[c]
        m_new = jnp.maximum(b_last + m_prev, ws_max)
        cn_ref[...] = jnp.exp(b_last + m_prev - m_new) * cn + jnp.exp(ws_max - m_new) * kn0_ref[c]
        mb_ref[...] = jnp.broadcast_to(m_new, mb_ref.shape)

        for hd in heads:
            lo, hi = hd * HEAD_DIM_AB, (hd + 1) * HEAD_DIM_AB
            z_gate = proj_ref[rows, 3 * QK_AB + lo:3 * QK_AB + hi]
            out_ref[rows, lo:hi] = (_rms(o[hd], norm_a) * z_gate).astype(out_ref.dtype)
            o_gate = proj_ref[rows, 7 * QK_AB + lo:7 * QK_AB + hi]
            out_ref[rows, QK_AB + lo:QK_AB + hi] = (
                _rms(hout[hd], normb_ref[hd:hd + 1, :]) * o_gate).astype(out_ref.dtype)
        return carry

    lax.fori_loop(0, n_chunks, chunk_body, 0, unroll=SEQ_UNROLL)


def _mixer_ab(h, g0, w_in, conv_w, a_log, dt_bias, norm_a, i_bias, f_bias, norm_b, batch, seq):
    tokens = batch * seq
    ts = min(SEQ_TILE, seq)
    tiles = seq // ts
    a_cols = 2 * QK_AB + QK_AB + 2 * HEADS_AB + QK_AB
    wa, wb = w_in[:, :a_cols], w_in[:, a_cols:]
    g_off = 3 * QK_AB
    z_off = g_off + 2 * HEADS_AB
    w_big = jnp.concatenate([wa[:, :g_off], wa[:, z_off:], wb[:, :g_off], wb[:, z_off:]], 1).astype(BF16)
    w_gate = jnp.concatenate([wa[:, g_off:z_off], wb[:, g_off:z_off],
                              jnp.zeros((D_MODEL, GATE_COLS - 4 * HEADS_AB), F32)], 1).astype(BF16)
    zeros4 = jnp.zeros((HEADS_AB,), F32)
    pad = jnp.zeros((GATE_COLS - 4 * HEADS_AB,), F32)
    bias_row = jnp.concatenate([zeros4, dt_bias, i_bias, f_bias, pad])
    alog_row = jnp.concatenate([zeros4, a_log, zeros4, zeros4, pad])
    gprm = jnp.concatenate([bias_row[None], alog_row[None], jnp.zeros((6, GATE_COLS), F32)], 0)

    const = lambda shape: pl.BlockSpec(shape, lambda b, t: (0,) * len(shape),
                                       pipeline_mode=pl.Buffered(1))
    n_chunks = ts // CHUNK
    per_head = (n_chunks, HEADS_AB)
    return pl.pallas_call(
        _mixer_ab_kernel,
        grid=(batch, tiles),
        in_specs=[
            pl.BlockSpec((ts, D_MODEL), lambda b, t: (b * tiles + t, 0)),
            const((1, D_MODEL)),
            const((D_MODEL, BIG_COLS)),
            const((D_MODEL, GATE_COLS)),
            const((CONV_K, CONV_CH)),
            const((8, GATE_COLS)),
            const((1, HEAD_DIM_AB)),
            const((HEADS_AB, HEAD_DIM_AB)),
        ],
        out_specs=pl.BlockSpec((ts, 2 * QK_AB), lambda b, t: (b * tiles + t, 0)),
        out_shape=jax.ShapeDtypeStruct((tokens, 2 * QK_AB), BF16),
        scratch_shapes=[
            pltpu.VMEM((ts, BIG_COLS), F32),
            pltpu.VMEM((ts + 8, CONV_CH), F32),
            pltpu.VMEM((ts, CONV_CH), F32),
            pltpu.VMEM((ts, GATE_COLS), F32),
            pltpu.VMEM((ts, GATE_COLS), F32),
            pltpu.VMEM((n_chunks, 16, 2 * CHUNK), F32),
            pltpu.VMEM((n_chunks, 16, 2 * CHUNK), F32),
            pltpu.VMEM((HEADS_AB, ts, HEAD_DIM_AB), F32),
            pltpu.VMEM((HEADS_AB, ts, HEAD_DIM_AB), F32),
            pltpu.VMEM(per_head + (CHUNK, HEAD_DIM_AB), F32),
            pltpu.VMEM(per_head + (2 * CHUNK, HEAD_DIM_AB), BF16),
            pltpu.VMEM(per_head + (CHUNK, CHUNK), BF16),
            pltpu.VMEM(per_head + (HEAD_DIM_AB, CHUNK), BF16),
            pltpu.VMEM(per_head + (1, HEAD_DIM_AB), F32),
            pltpu.VMEM(per_head + (CHUNK, HEAD_DIM_AB), BF16),
            pltpu.VMEM(per_head + (CHUNK, HEAD_DIM_AB), F32),
            pltpu.VMEM(per_head + (CHUNK, HEAD_DIM_AB), F32),
            pltpu.VMEM(per_head + (CHUNK, HEAD_DIM_AB), F32),
            pltpu.VMEM(per_head + (HEAD_DIM_AB, 2 * HEAD_DIM_AB), F32),
            pltpu.VMEM(per_head + (1, 2 * HEAD_DIM_AB), F32),
            pltpu.VMEM(per_head + (1, 2 * HEAD_DIM_AB), F32),
            pltpu.VMEM((HEADS_AB, HEAD_DIM_AB, HEAD_DIM_AB), F32),
            pltpu.VMEM((HEADS_AB, HEAD_DIM_AB, 2 * HEAD_DIM_AB), F32),
            pltpu.VMEM((HEADS_AB, 8, 2 * HEAD_DIM_AB), F32),
        ],
        compiler_params=pltpu.CompilerParams(
            dimension_semantics=("arbitrary", "arbitrary"),
            vmem_limit_bytes=VMEM_LIMIT_BYTES),
        name="mixer_ab",
    )(h, g0[None], w_big, w_gate, conv_w, gprm, norm_a[None], norm_b)


def _mixer_c_kernel(h_ref, pos_ref, invc_ref, sink_ref, g0_ref, wqt_ref, bq_ref, wkt_ref, bk_ref,
                    wvt_ref, bv_ref, out_ref, q_ref, klo_ref, khi_ref, vtlo_ref, vthi_ref):
    ts = h_ref.shape[0]
    n_blocks = ts // WINDOW
    half = C_HD // 2
    t = pl.program_id(1)

    @pl.when(t == 0)
    def _():
        vtlo_ref[...] = jnp.zeros_like(vtlo_ref)
        vthi_ref[...] = jnp.zeros_like(vthi_ref)
        klo_ref[0:WINDOW, :] = jnp.zeros((WINDOW, klo_ref.shape[1]), BF16)
        khi_ref[0:WINDOW, :] = jnp.zeros((WINDOW, khi_ref.shape[1]), BF16)

    hn = _rms(h_ref[...], g0_ref[...]).astype(BF16)

    ang_t = pos_ref[...].astype(F32) * invc_ref[...]
    cos_t = jnp.cos(ang_t)
    sin_t = jnp.sin(ang_t)

    def rope_t(x):
        x1, x2 = x[0:half], x[half:C_HD]
        return jnp.concatenate([x1 * cos_t - x2 * sin_t, x2 * cos_t + x1 * sin_t], 0)

    qt = _dot_nt(wqt_ref[...], hn) + bq_ref[...]
    for hd in range(C_HEADS):
        rows = slice(hd * C_HD, (hd + 1) * C_HD)
        q_ref[rows, :] = (rope_t(qt[rows]) * (C_HD ** -0.5)).astype(BF16)

    kt = _dot_nt(wkt_ref[...], hn) + bk_ref[...]
    kk = jnp.concatenate([rope_t(kt[j * C_HD:(j + 1) * C_HD]) for j in range(2 * C_KV_HEADS)], 0).T
    low = (lax.broadcasted_iota(jnp.int32, (1, 2 * C_KV), 1) % 128) < C_HD
    klo_ref[WINDOW:, :] = jnp.where(low, kk, 0.0).astype(BF16)
    khi_ref[WINDOW:, :] = jnp.where(low, 0.0, kk).astype(BF16)

    vt = (_dot_nt(wvt_ref[...], hn) + bv_ref[...]).astype(BF16)
    for g in range(C_KV_HEADS):
        vtlo_ref[2 * g * C_HD:(2 * g + 1) * C_HD, WINDOW:] = vt[g * C_HD:(g + 1) * C_HD]
        vthi_ref[(2 * g + 1) * C_HD:(2 * g + 2) * C_HD, WINDOW:] = vt[g * C_HD:(g + 1) * C_HD]

    bdot = lambda x, y: lax.dot_general(x, y, (((2,), (1,)), ((0,), (0,))), preferred_element_type=F32)
    groups = range(C_KV_HEADS)
    key_idx = lax.broadcasted_iota(jnp.int32, (WINDOW, 2 * WINDOW), 0)
    qry_idx = lax.broadcasted_iota(jnp.int32, (WINDOW, 2 * WINDOW), 1) % WINDOW
    from_prev = key_idx > qry_idx
    def scores(blk):
        cols = slice(blk * WINDOW, (blk + 1) * WINDOW)
        win = slice(blk * WINDOW, (blk + 2) * WINDOW)
        kbd = jnp.stack([jnp.concatenate([klo_ref[win, g * 128:(g + 1) * 128],
                                          khi_ref[win, g * 128:(g + 1) * 128]], 0) for g in groups])
        qbd = jnp.stack([jnp.concatenate([q_ref[g * 256:g * 256 + 128, cols],
                                          q_ref[g * 256 + 128:(g + 1) * 256, cols]], 1) for g in groups])
        return bdot(kbd, qbd)

    pending = [scores(blk) for blk in range(min(SCORES_AHEAD, n_blocks))]
    for blk in range(n_blocks):
        cols = slice(blk * WINDOW, (blk + 1) * WINDOW)
        win = slice(blk * WINDOW, (blk + 2) * WINDOW)
        st = pending.pop(0)
        if blk + SCORES_AHEAD < n_blocks:
            pending.append(scores(blk + SCORES_AHEAD))
        probs = []
        for odd in range(2):
            s_prev = st[:, odd * 2 * WINDOW:odd * 2 * WINDOW + WINDOW]
            s_cur = st[:, odd * 2 * WINDOW + WINDOW:(odd + 1) * 2 * WINDOW]
            sc = jnp.where(from_prev, s_prev, s_cur)
            if blk == 0:
                sc = jnp.where(from_prev & (key_idx < jnp.where(t > 0, -1, WINDOW)), -jnp.inf, sc)
            sink = sink_ref[odd]
            m = jnp.maximum(jnp.max(sc, 1, keepdims=True), sink)
            pr = jnp.exp(sc - m)
            den = jnp.sum(pr, 1, keepdims=True) + jnp.exp(sink - m)
            pr = pr / den
            probs += [jnp.where(from_prev, pr, 0.0).astype(BF16), jnp.where(from_prev, 0.0, pr).astype(BF16)]
        vbd = jnp.stack([jnp.concatenate([vtlo_ref[g * 128:(g + 1) * 128, win],
                                          vthi_ref[g * 128:(g + 1) * 128, win]], 1) for g in groups])
        ot = bdot(vbd, jnp.concatenate(probs, 1))
        for g in groups:
            o = ot[g].T
            out_ref[cols, g * 256:g * 256 + 128] = o[0:WINDOW].astype(out_ref.dtype)
            out_ref[cols, g * 256 + 128:(g + 1) * 256] = o[WINDOW:].astype(out_ref.dtype)

    klo_ref[0:WINDOW, :] = klo_ref[ts:ts + WINDOW, :]
    khi_ref[0:WINDOW, :] = khi_ref[ts:ts + WINDOW, :]
    vtlo_ref[:, 0:WINDOW] = vtlo_ref[:, ts:ts + WINDOW]
    vthi_ref[:, 0:WINDOW] = vthi_ref[:, ts:ts + WINDOW]


def _mixer_c(h, g0, positions, w_qkv, b_qkv, sinks, batch, seq):
    tokens = batch * seq
    ts = min(SEQ_TILE, seq)
    tiles = seq // ts
    half = C_HD // 2
    inv_col = (ROPE_THETA ** (-jnp.arange(half, dtype=F32) / half))[:, None]
    pos = positions.reshape(batch * tiles, 1, ts)
    wt, b_col = w_qkv.T.astype(BF16), b_qkv[:, None]
    w_q, b_q = wt[:C_Q], b_col[:C_Q]
    dup = lambda a: jnp.repeat(a.reshape(C_KV_HEADS, 1, C_HD, -1), 2, 1).reshape(2 * C_KV, -1)
    w_k, b_k = dup(wt[C_Q:C_Q + C_KV]), dup(b_col[C_Q:C_Q + C_KV])
    w_v, b_v = wt[C_Q + C_KV:], b_col[C_Q + C_KV:]
    sink_rows = jnp.repeat(sinks.reshape(C_KV_HEADS, 2, 2).transpose(2, 0, 1), WINDOW, -1)[:, :, None, :]
    const = lambda shape: pl.BlockSpec(shape, lambda b, t: (0,) * len(shape),
                                       pipeline_mode=pl.Buffered(1))
    return pl.pallas_call(
        _mixer_c_kernel,
        grid=(batch, tiles),
        in_specs=[
            pl.BlockSpec((ts, D_MODEL), lambda b, t: (b * tiles + t, 0)),
            pl.BlockSpec((None, 1, ts), lambda b, t: (b * tiles + t, 0, 0)),
            const((half, 1)),
            const((2, C_KV_HEADS, 1, 2 * WINDOW)),
            const((1, D_MODEL)),
            const((C_Q, D_MODEL)),
            const((C_Q, 1)),
            const((2 * C_KV, D_MODEL)),
            const((2 * C_KV, 1)),
            const((C_KV, D_MODEL)),
            const((C_KV, 1)),
        ],
        out_specs=pl.BlockSpec((ts, C_Q), lambda b, t: (b * tiles + t, 0)),
        out_shape=jax.ShapeDtypeStruct((tokens, C_Q), BF16),
        scratch_shapes=[
            pltpu.VMEM((C_Q, ts), BF16),
            pltpu.VMEM((WINDOW + ts, 2 * C_KV), BF16),
            pltpu.VMEM((WINDOW + ts, 2 * C_KV), BF16),
            pltpu.VMEM((2 * C_KV, WINDOW + ts), BF16),
            pltpu.VMEM((2 * C_KV, WINDOW + ts), BF16),
        ],
        compiler_params=pltpu.CompilerParams(
            dimension_semantics=("arbitrary", "arbitrary"),
            vmem_limit_bytes=VMEM_LIMIT_BYTES),
        name="mixer_c",
    )(h, pos, inv_col, sink_rows, g0[None], w_q, b_q, w_k, b_k, w_v, b_v)


def _post_kernel(mix_ref, h_ref, p_ref, wout_ref, bout_ref, gains_ref, wup_ref, wdown_ref,
                 wgate_ref, wple_ref, out_ref):
    mix = jnp.dot(mix_ref[...], wout_ref[...], preferred_element_type=F32) + bout_ref[...]
    h = h_ref[...] + _rms(mix, gains_ref[1:2, :])
    hn = _rms(h, gains_ref[2:3, :]).astype(BF16)
    slab = D_FF // FF_SPLIT
    ff = None
    for j in range(FF_SPLIT):
        up = jnp.dot(hn, wup_ref[:, j * slab:(j + 1) * slab], preferred_element_type=F32)
        act = jnp.square(jnp.maximum(up, 0.0)).astype(BF16)
        part = jnp.dot(act, wdown_ref[j * slab:(j + 1) * slab, :], preferred_element_type=F32)
        ff = part if ff is None else ff + part
    h = h + _rms(ff, gains_ref[3:4, :])
    gate = _sigmoid(jnp.dot(h.astype(BF16), wgate_ref[...], preferred_element_type=F32))
    ple = jnp.dot(p_ref[...].astype(BF16), wple_ref[...], preferred_element_type=F32)
    out_ref[...] = h + gate * ple


def _post(mixed, h, p_all, layer, w_out, b_out, gains, w_up, w_down, w_gate, w_ple):
    tokens = h.shape[0]
    tm = min(POST_TILE, tokens)
    const = lambda shape: pl.BlockSpec(shape, lambda i: (0,) * len(shape),
                                       pipeline_mode=pl.Buffered(1))
    return pl.pallas_call(
        _post_kernel,
        grid=(tokens // tm,),
        in_specs=[
            pl.BlockSpec((tm, D_MODEL), lambda i: (i, 0)),
            pl.BlockSpec((tm, D_MODEL), lambda i: (i, 0)),
            pl.BlockSpec((None, tm, PLE_DIM), lambda i: (layer, i, 0)),
            const((D_MODEL, D_MODEL)),
            const((1, D_MODEL)),
            const((4, D_MODEL)),
            const((D_MODEL, D_FF)),
            const((D_FF, D_MODEL)),
            const((D_MODEL, D_MODEL)),
            const((PLE_DIM, D_MODEL)),
        ],
        out_specs=pl.BlockSpec((tm, D_MODEL), lambda i: (i, 0)),
        out_shape=jax.ShapeDtypeStruct((tokens, D_MODEL), F32),
        compiler_params=pltpu.CompilerParams(
            dimension_semantics=("arbitrary",),
            vmem_limit_bytes=VMEM_LIMIT_BYTES),
        name="post",
    )(mixed, h, p_all, w_out.astype(BF16), b_out[None], gains, w_up.astype(BF16),
      w_down.astype(BF16), w_gate.astype(BF16), w_ple.astype(BF16))


def kernel(x, p, positions, norm_gains, w_in_ab, conv_a, a_log, dt_bias, norm_a, i_bias_b, f_bias_b, norm_b, w_out_ab, w_qkv_c, b_qkv_c, sinks_c, w_o_c, b_o_c, w_up, w_down, w_ple, w_ple_gate):
    batch, seq, _ = x.shape
    depth = norm_gains.shape[0]
    tokens = batch * seq
    h = x.reshape(tokens, D_MODEL)
    p_all = p.reshape(depth, tokens, PLE_DIM)
    for layer in range(depth):
        gains = norm_gains[layer]
        if layer % 2 == 0:
            e = layer // 2
            mixed = _mixer_ab(h, gains[0], w_in_ab[e], conv_a[e], a_log[e], dt_bias[e], norm_a[e],
                              i_bias_b[e], f_bias_b[e], norm_b[e], batch, seq)
            w_out, b_out = w_out_ab[e], jnp.zeros((D_MODEL,), F32)
        else:
            o = layer // 2
            mixed = _mixer_c(h, gains[0], positions, w_qkv_c[o], b_qkv_c[o], sinks_c[o], batch, seq)
            w_out, b_out = w_o_c[o], b_o_c[o]
        h = _post(mixed, h, p_all, layer, w_out, b_out, gains, w_up[layer], w_down[layer],
                  w_ple_gate[layer], w_ple[layer])
    return h.reshape(batch, seq, D_MODEL)
```

```python
import functools

import jax
import jax.numpy as jnp
from jax import lax
from jax.experimental import pallas as pl
from jax.experimental.pallas import tpu as pltpu

F32 = jnp.float32
BF16 = jnp.bfloat16
HIGHEST = lax.Precision.HIGHEST

EPS = 1e-6
D_MODEL = 1024
HEADS_AB = 4
HEAD_DIM_AB = 128
CONV_K = 4
CHUNK = 64
C_HEADS = 16
C_KV_HEADS = 4
C_GROUP = C_HEADS // C_KV_HEADS
C_HD = 64
WINDOW = 128
ROPE_THETA = 10000.0
D_FF = 4 * D_MODEL
PLE_DIM = 256
QK_AB = HEADS_AB * HEAD_DIM_AB
CONV_CH = 3 * QK_AB
BIG_COLS = 8 * QK_AB
GATE_COLS = 128
C_Q = C_HEADS * C_HD
C_KV = C_KV_HEADS * C_HD

AB_ROWS = 2
AB_TILE = 256
C_TILE = 2048
POST_TILE = 512
PREP_CHUNKS = 8
SCORES_AHEAD = 4
SEQ_UNROLL = 4
FF_SPLIT = 4
VMEM_LIMIT_BYTES = 56 * 1024 * 1024


def _rms(x, g):
    return x * lax.rsqrt(jnp.mean(x * x, -1, keepdims=True) + EPS) * g


def _dot(a, b):
    return jnp.dot(a.astype(BF16), b.astype(BF16), preferred_element_type=F32)


def _dot_nt(a, b):
    return lax.dot_general(a.astype(BF16), b.astype(BF16), (((1,), (1,)), ((), ())),
                           preferred_element_type=F32)


def _dot_tn(a, b):
    return lax.dot_general(a.astype(BF16), b.astype(BF16), (((0,), (0,)), ((), ())),
                           preferred_element_type=F32)


def _dot_hi(a, b):
    return jnp.dot(a, b, preferred_element_type=F32, precision=HIGHEST)


def _split(a):
    hi = a.astype(BF16)
    hi_f = hi.astype(F32)
    lo_f = a - hi_f
    left = lax.broadcasted_iota(jnp.int32, a.shape, 2) < a.shape[2] // 2
    return hi, lo_f.astype(BF16), jnp.where(left, hi_f, lo_f).astype(BF16)


def _dot3(a, b):
    (ah, _, ahl), (bh, bl, _) = a, b
    d = lambda x, y: lax.dot_general(x, y, (((2,), (1,)), ((0,), (0,))), preferred_element_type=F32)
    return d(ahl, jnp.concatenate([bl, bh], 1)) + d(ah[:, :, 0:ah.shape[2] // 2], bh)


def _softplus(x):
    return jnp.maximum(x, 0.0) + jnp.log1p(jnp.exp(-jnp.abs(x)))


def _sigmoid(x):
    return 1.0 / (1.0 + jnp.exp(-x))


def _silu(x):
    return x * _sigmoid(x)


def _mixer_ab_kernel(h_ref, g0_ref, wbig_ref, wgate_ref, convw_ref, gprm_ref, norma_ref,
                     normb_ref, out_ref,
                     proj_ref, xbuf_ref, qkv_ref, gate_ref, gcum_ref, gt_ref, gcumt_ref, gcb_ref, bcb_ref,
                     u_ref, wq_ref, attn_ref, kdect_ref, eg_ref,
                     qbb_ref, pv0_ref, rs0_ref, mib_ref, kn0_ref, bl_ref, wm_ref,
                     sa_ref, cn_ref, mb_ref):
    n_rows, tile = h_ref.shape[0], h_ref.shape[1]
    ts = n_rows * tile
    n_chunks = ts // CHUNK
    chunks_per_row = tile // CHUNK
    t = pl.program_id(1)

    @pl.when(t == 0)
    def _():
        xbuf_ref[:, 0:8, :] = jnp.zeros((n_rows, 8, CONV_CH), F32)
        sa_ref[...] = jnp.zeros_like(sa_ref)
        cn_ref[...] = jnp.zeros_like(cn_ref)
        mb_ref[...] = jnp.zeros_like(mb_ref)

    hn = _rms(h_ref[...].reshape(ts, D_MODEL), g0_ref[...]).astype(BF16)
    gates_pre = jnp.dot(hn, wgate_ref[...], preferred_element_type=F32) + gprm_ref[0:1, :]

    lane = lax.broadcasted_iota(jnp.int32, (1, GATE_COLS), 1)
    neg_a = -jnp.exp(gprm_ref[1:2, :])
    gate = jnp.where(lane < 4, _sigmoid(gates_pre),
                     jnp.where(lane < 8, neg_a * _softplus(gates_pre),
                               jnp.where(lane < 12, gates_pre, -_softplus(-gates_pre))))
    gate_ref[...] = gate
    row = lax.broadcasted_iota(jnp.int32, (CHUNK, CHUNK), 0)
    col = lax.broadcasted_iota(jnp.int32, (CHUNK, CHUNK), 1)
    causal = row >= col
    tri = causal.astype(F32)
    row2 = lax.broadcasted_iota(jnp.int32, (CHUNK, 2 * CHUNK), 0)
    col2 = lax.broadcasted_iota(jnp.int32, (CHUNK, 2 * CHUNK), 1) % CHUNK
    causal2 = row2 >= col2
    strict2 = row2 > col2
    eye2 = (row2 == col2).astype(F32)
    twice = lambda x: jnp.concatenate([x, x], 1)
    for c in range(n_chunks):
        gch = gate_ref[c * CHUNK:(c + 1) * CHUNK, :]
        cum = _dot_hi(tri, gch)
        gcum_ref[c * CHUNK:(c + 1) * CHUNK, :] = cum
        gt_ref[c] = twice(gch.T[0:16, :])
        gcumt_ref[c] = twice(cum.T[0:16, :])
    for hd in range(HEADS_AB):
        gcb_ref[hd] = jnp.broadcast_to(gcum_ref[:, 4 + hd:5 + hd], (ts, HEAD_DIM_AB))
        bcb_ref[hd] = jnp.broadcast_to(gcum_ref[:, 12 + hd:13 + hd], (ts, HEAD_DIM_AB))

    proj_ref[...] = jnp.dot(hn, wbig_ref[...], preferred_element_type=F32)
    proj_ref[:, 3 * QK_AB:4 * QK_AB] = _silu(proj_ref[:, 3 * QK_AB:4 * QK_AB])
    proj_ref[:, 7 * QK_AB:8 * QK_AB] = _sigmoid(proj_ref[:, 7 * QK_AB:8 * QK_AB])

    for r in range(n_rows):
        xbuf_ref[r, 8:8 + tile, :] = proj_ref[r * tile:(r + 1) * tile, 0:CONV_CH]
        xfull = xbuf_ref[r]
        conv = convw_ref[0:1, :] * xfull
        for k in range(1, CONV_K):
            conv = convw_ref[k:k + 1, :] * xfull + pltpu.roll(conv, 1, 0)
        xbuf_ref[r, 0:8, :] = xbuf_ref[r, tile:tile + 8, :]
        qkv_ref[r * tile:(r + 1) * tile, :] = _silu(conv[8:, :])
    for hd in range(HEADS_AB):
        sl = slice(hd * HEAD_DIM_AB, (hd + 1) * HEAD_DIM_AB)
        q = qkv_ref[:, sl]
        qkv_ref[:, sl] = q * lax.rsqrt(jnp.sum(q * q, -1, keepdims=True) + EPS) * (HEAD_DIM_AB ** -0.5)
        sl = slice(QK_AB + hd * HEAD_DIM_AB, QK_AB + (hd + 1) * HEAD_DIM_AB)
        k = qkv_ref[:, sl]
        qkv_ref[:, sl] = k * lax.rsqrt(jnp.sum(k * k, -1, keepdims=True) + EPS)

    norm_a = norma_ref[...]

    def prep_body(i, carry):
        lmats, rhs = [], []
        for j in range(PREP_CHUNKS):
            c = i * PREP_CHUNKS + j
            r0 = pl.multiple_of(c * CHUNK, CHUNK)
            rows = pl.ds(r0, CHUNK)
            gct = gcumt_ref[c]
            for hd in range(HEADS_AB):
                lo, hi = hd * HEAD_DIM_AB, (hd + 1) * HEAD_DIM_AB
                q = qkv_ref[rows, lo:hi]
                k = qkv_ref[rows, QK_AB + lo:QK_AB + hi]
                v = qkv_ref[rows, 2 * QK_AB + lo:2 * QK_AB + hi]
                gtc = gt_ref[c]
                beta_row = gtc[hd:hd + 1, :]
                gc_row = gct[4 + hd:5 + hd, :]
                gc_all = gcb_ref[hd, rows, :]
                g_last = gc_all[CHUNK - 1:CHUNK, :]
                diff = gc_all - gc_row
                decay = jnp.where(causal2, jnp.exp(jnp.where(causal2, diff, 0.0)), 0.0)
                egc = jnp.exp(gc_all)
                k_bf = k.astype(BF16)
                kk = _dot_nt(k_bf, jnp.concatenate([k_bf, k_bf], 0))
                lmats.append(jnp.where(strict2, kk * decay, 0.0) * beta_row)
                rhs.append((c, hd, v.astype(BF16), (k * egc).astype(BF16)))
                wq_ref[c, hd, CHUNK:2 * CHUNK, :] = (q * egc).astype(BF16)
                attn_ref[c, hd] = (_dot_nt(q, k_bf) * (decay * beta_row)[:, 0:CHUNK]).astype(BF16)
                kdect_ref[c, hd] = ((k * jnp.exp(g_last - gc_all)).T * beta_row[:, 0:CHUNK]).astype(BF16)
                eg_ref[c, hd] = jnp.exp(g_last)
                qb = proj_ref[rows, 4 * QK_AB + lo:4 * QK_AB + hi] * (HEAD_DIM_AB ** -0.5)
                kbb = proj_ref[rows, 5 * QK_AB + lo:5 * QK_AB + hi]
                vbb = proj_ref[rows, 6 * QK_AB + lo:6 * QK_AB + hi].astype(BF16)
                bc_all = bcb_ref[hd, rows, :]
                bc_row = gct[12 + hd:13 + hd, 0:CHUNK]
                i_row = gtc[8 + hd:9 + hd, 0:CHUNK]
                b_last = bc_all[CHUNK - 1:CHUNK, :]
                dmat = jnp.where(causal, bc_all[:, 0:CHUNK] - bc_row + i_row, -jnp.inf)
                m_intra = jnp.max(dmat, -1, keepdims=True)
                p0 = _dot_nt(qb, kbb) * jnp.exp(dmat - m_intra)
                qbb_ref[c, hd] = qb.astype(BF16)
                pv0_ref[c, hd] = _dot(p0, vbb)
                rs0_ref[c, hd] = jnp.broadcast_to(jnp.sum(p0, -1, keepdims=True), (CHUNK, HEAD_DIM_AB))
                mib_ref[c, hd] = jnp.broadcast_to(m_intra, (CHUNK, HEAD_DIM_AB))
                ws_row = b_last[:, 0:CHUNK] - bc_row + i_row
                ws_max = jnp.max(ws_row, -1, keepdims=True)
                kw_t = kbb.T * jnp.exp(ws_row - ws_max)
                kn0_ref[c, hd, :, 0:HEAD_DIM_AB] = _dot(kw_t, vbb)
                kn0_ref[c, hd, :, HEAD_DIM_AB:] = jnp.broadcast_to(
                    jnp.sum(kw_t, -1, keepdims=True), (HEAD_DIM_AB, HEAD_DIM_AB))
                bl_ref[c, hd] = jnp.concatenate([b_last, b_last], 1)
                wm_ref[c, hd] = jnp.broadcast_to(ws_max, (1, 2 * HEAD_DIM_AB))
        lmat = jnp.stack(lmats)
        l_split = _split(lmat)
        tinv = eye2 - lmat
        pw = _dot3(l_split, l_split)
        for _ in range(4):
            pw_split = _split(pw)
            tinv = tinv + _dot3(_split(tinv), pw_split)
            pw = _dot3(pw_split, pw_split)
        tinv = (tinv + _dot3(_split(tinv), _split(pw)))[:, :, 0:CHUNK].astype(BF16)
        for n, (c, hd, v_bf, ke_bf) in enumerate(rhs):
            u_ref[c, hd] = jnp.dot(tinv[n], v_bf, preferred_element_type=F32)
            wq_ref[c, hd, 0:CHUNK, :] = jnp.dot(tinv[n], ke_bf, preferred_element_type=F32).astype(BF16)
        return carry

    lax.fori_loop(0, n_chunks // PREP_CHUNKS, prep_body, 0)

    def chunk_body(j, carry):
        local = pl.ds(pl.multiple_of(j * CHUNK, CHUNK), CHUNK)
        rows = [pl.ds(pl.multiple_of(r * tile + j * CHUNK, CHUNK), CHUNK) for r in range(n_rows)]
        both = lambda ref: jnp.concatenate([ref[r * chunks_per_row + j] for r in range(n_rows)], 0)
        bdot = lambda x, y: lax.dot_general(x, y, (((2,), (1,)), ((0,), (0,))),
                                            preferred_element_type=F32)
        state = sa_ref[...]
        ws_qs = bdot(both(wq_ref), state.astype(BF16))
        cn = cn_ref[...]
        q_cn = bdot(both(qbb_ref), cn.astype(BF16))

        v_new = (both(u_ref) - ws_qs[:, 0:CHUNK]).astype(BF16)
        o = ws_qs[:, CHUNK:] + bdot(both(attn_ref), v_new)
        sa_ref[...] = state * both(eg_ref) + bdot(both(kdect_ref), v_new)

        m_prev = mb_ref[:, 0:1, :]
        a = jnp.concatenate([bcb_ref[:, rw, :] for rw in rows], 0) + m_prev[:, :, 0:HEAD_DIM_AB]
        m_intra = both(mib_ref)
        mt = jnp.maximum(a, m_intra)
        inter = jnp.exp(a - mt)
        intra = jnp.exp(m_intra - mt)
        num = inter * q_cn[:, :, 0:HEAD_DIM_AB] + intra * both(pv0_ref)
        den = inter * q_cn[:, :, HEAD_DIM_AB:] + intra * both(rs0_ref)
        hout = num / jnp.maximum(jnp.abs(den), jnp.exp(-mt))
        b_last, ws_max = both(bl_ref), both(wm_ref)
        m_new = jnp.maximum(b_last + m_prev, ws_max)
        cn_ref[...] = jnp.exp(b_last + m_prev - m_new) * cn + jnp.exp(ws_max - m_new) * both(kn0_ref)
        mb_ref[...] = jnp.broadcast_to(m_new, mb_ref.shape)

        for r in range(n_rows):
            for hd in range(HEADS_AB):
                n = r * HEADS_AB + hd
                lo, hi = hd * HEAD_DIM_AB, (hd + 1) * HEAD_DIM_AB
                z_gate = proj_ref[rows[r], 3 * QK_AB + lo:3 * QK_AB + hi]
                out_ref[r, local, lo:hi] = (_rms(o[n], norm_a) * z_gate).astype(out_ref.dtype)
                o_gate = proj_ref[rows[r], 7 * QK_AB + lo:7 * QK_AB + hi]
                out_ref[r, local, QK_AB + lo:QK_AB + hi] = (
                    _rms(hout[n], normb_ref[hd:hd + 1, :]) * o_gate).astype(out_ref.dtype)
        return carry

    lax.fori_loop(0, chunks_per_row, chunk_body, 0, unroll=SEQ_UNROLL)


def _mixer_ab(h, g0, w_in, conv_w, a_log, dt_bias, norm_a, i_bias, f_bias, norm_b, batch, seq):
    tile = min(AB_TILE, seq)
    n_rows = AB_ROWS if batch % AB_ROWS == 0 else 1
    tiles = seq // tile
    ts = n_rows * tile
    a_cols = 2 * QK_AB + QK_AB + 2 * HEADS_AB + QK_AB
    wa, wb = w_in[:, :a_cols], w_in[:, a_cols:]
    g_off = 3 * QK_AB
    z_off = g_off + 2 * HEADS_AB
    w_big = jnp.concatenate([wa[:, :g_off], wa[:, z_off:], wb[:, :g_off], wb[:, z_off:]], 1).astype(BF16)
    w_gate = jnp.concatenate([wa[:, g_off:z_off], wb[:, g_off:z_off],
                              jnp.zeros((D_MODEL, GATE_COLS - 4 * HEADS_AB), F32)], 1).astype(BF16)
    zeros4 = jnp.zeros((HEADS_AB,), F32)
    pad = jnp.zeros((GATE_COLS - 4 * HEADS_AB,), F32)
    bias_row = jnp.concatenate([zeros4, dt_bias, i_bias, f_bias, pad])
    alog_row = jnp.concatenate([zeros4, a_log, zeros4, zeros4, pad])
    gprm = jnp.concatenate([bias_row[None], alog_row[None], jnp.zeros((6, GATE_COLS), F32)], 0)

    const = lambda shape: pl.BlockSpec(shape, lambda b, t: (0,) * len(shape),
                                       pipeline_mode=pl.Buffered(1))
    n_chunks = ts // CHUNK
    per_head = (n_chunks, HEADS_AB)
    all_heads = n_rows * HEADS_AB
    return pl.pallas_call(
        _mixer_ab_kernel,
        grid=(batch // n_rows, tiles),
        in_specs=[
            pl.BlockSpec((n_rows, tile, D_MODEL), lambda b, t: (b, t, 0)),
            const((1, D_MODEL)),
            const((D_MODEL, BIG_COLS)),
            const((D_MODEL, GATE_COLS)),
            const((CONV_K, CONV_CH)),
            const((8, GATE_COLS)),
            const((1, HEAD_DIM_AB)),
            const((HEADS_AB, HEAD_DIM_AB)),
        ],
        out_specs=pl.BlockSpec((n_rows, tile, 2 * QK_AB), lambda b, t: (b, t, 0)),
        out_shape=jax.ShapeDtypeStruct((batch, seq, 2 * QK_AB), BF16),
        scratch_shapes=[
            pltpu.VMEM((ts, BIG_COLS), F32),
            pltpu.VMEM((n_rows, tile + 8, CONV_CH), F32),
            pltpu.VMEM((ts, CONV_CH), F32),
            pltpu.VMEM((ts, GATE_COLS), F32),
            pltpu.VMEM((ts, GATE_COLS), F32),
            pltpu.VMEM((n_chunks, 16, 2 * CHUNK), F32),
            pltpu.VMEM((n_chunks, 16, 2 * CHUNK), F32),
            pltpu.VMEM((HEADS_AB, ts, HEAD_DIM_AB), F32),
            pltpu.VMEM((HEADS_AB, ts, HEAD_DIM_AB), F32),
            pltpu.VMEM(per_head + (CHUNK, HEAD_DIM_AB), F32),
            pltpu.VMEM(per_head + (2 * CHUNK, HEAD_DIM_AB), BF16),
            pltpu.VMEM(per_head + (CHUNK, CHUNK), BF16),
            pltpu.VMEM(per_head + (HEAD_DIM_AB, CHUNK), BF16),
            pltpu.VMEM(per_head + (1, HEAD_DIM_AB), F32),
            pltpu.VMEM(per_head + (CHUNK, HEAD_DIM_AB), BF16),
            pltpu.VMEM(per_head + (CHUNK, HEAD_DIM_AB), F32),
            pltpu.VMEM(per_head + (CHUNK, HEAD_DIM_AB), F32),
            pltpu.VMEM(per_head + (CHUNK, HEAD_DIM_AB), F32),
            pltpu.VMEM(per_head + (HEAD_DIM_AB, 2 * HEAD_DIM_AB), F32),
            pltpu.VMEM(per_head + (1, 2 * HEAD_DIM_AB), F32),
            pltpu.VMEM(per_head + (1, 2 * HEAD_DIM_AB), F32),
            pltpu.VMEM((all_heads, HEAD_DIM_AB, HEAD_DIM_AB), F32),
            pltpu.VMEM((all_heads, HEAD_DIM_AB, 2 * HEAD_DIM_AB), F32),
            pltpu.VMEM((all_heads, 8, 2 * HEAD_DIM_AB), F32),
        ],
        compiler_params=pltpu.CompilerParams(
            dimension_semantics=("arbitrary", "arbitrary"),
            vmem_limit_bytes=VMEM_LIMIT_BYTES),
        name="mixer_ab",
    )(h.reshape(batch, seq, D_MODEL), g0[None], w_big, w_gate, conv_w, gprm, norm_a[None],
      norm_b).reshape(batch * seq, 2 * QK_AB)


def _mixer_c_kernel(h_ref, pos_ref, invc_ref, sink_ref, g0_ref, wqt_ref, bq_ref, wkt_ref, bk_ref,
                    wvt_ref, bv_ref, out_ref, q_ref, klo_ref, khi_ref, vtlo_ref, vthi_ref):
    ts = h_ref.shape[0]
    n_blocks = ts // WINDOW
    half = C_HD // 2
    t = pl.program_id(1)

    @pl.when(t == 0)
    def _():
        vtlo_ref[...] = jnp.zeros_like(vtlo_ref)
        vthi_ref[...] = jnp.zeros_like(vthi_ref)
        klo_ref[0:WINDOW, :] = jnp.zeros((WINDOW, klo_ref.shape[1]), BF16)
        khi_ref[0:WINDOW, :] = jnp.zeros((WINDOW, khi_ref.shape[1]), BF16)

    hn = _rms(h_ref[...], g0_ref[...]).astype(BF16)

    ang_t = pos_ref[...].astype(F32) * invc_ref[...]
    cos_t = jnp.cos(ang_t)
    sin_t = jnp.sin(ang_t)

    def rope_t(x):
        x1, x2 = x[0:half], x[half:C_HD]
        return jnp.concatenate([x1 * cos_t - x2 * sin_t, x2 * cos_t + x1 * sin_t], 0)

    qt = _dot_nt(wqt_ref[...], hn) + bq_ref[...]
    for hd in range(C_HEADS):
        rows = slice(hd * C_HD, (hd + 1) * C_HD)
        q_ref[rows, :] = (rope_t(qt[rows]) * (C_HD ** -0.5)).astype(BF16)

    kt = _dot_nt(wkt_ref[...], hn) + bk_ref[...]
    kk = jnp.concatenate([rope_t(kt[j * C_HD:(j + 1) * C_HD]) for j in range(2 * C_KV_HEADS)], 0).T
    low = (lax.broadcasted_iota(jnp.int32, (1, 2 * C_KV), 1) % 128) < C_HD
    klo_ref[WINDOW:, :] = jnp.where(low, kk, 0.0).astype(BF16)
    khi_ref[WINDOW:, :] = jnp.where(low, 0.0, kk).astype(BF16)

    vt = (_dot_nt(wvt_ref[...], hn) + bv_ref[...]).astype(BF16)
    for g in range(C_KV_HEADS):
        vtlo_ref[2 * g * C_HD:(2 * g + 1) * C_HD, WINDOW:] = vt[g * C_HD:(g + 1) * C_HD]
        vthi_ref[(2 * g + 1) * C_HD:(2 * g + 2) * C_HD, WINDOW:] = vt[g * C_HD:(g + 1) * C_HD]

    bdot = lambda x, y: lax.dot_general(x, y, (((2,), (1,)), ((0,), (0,))), preferred_element_type=F32)
    groups = range(C_KV_HEADS)
    key_idx = lax.broadcasted_iota(jnp.int32, (WINDOW, 2 * WINDOW), 0)
    qry_idx = lax.broadcasted_iota(jnp.int32, (WINDOW, 2 * WINDOW), 1) % WINDOW
    from_prev = key_idx > qry_idx
    def scores(blk):
        cols = slice(blk * WINDOW, (blk + 1) * WINDOW)
        win = slice(blk * WINDOW, (blk + 2) * WINDOW)
        kbd = jnp.stack([jnp.concatenate([klo_ref[win, g * 128:(g + 1) * 128],
                                          khi_ref[win, g * 128:(g + 1) * 128]], 0) for g in groups])
        qbd = jnp.stack([jnp.concatenate([q_ref[g * 256:g * 256 + 128, cols],
                                          q_ref[g * 256 + 128:(g + 1) * 256, cols]], 1) for g in groups])
        return bdot(kbd, qbd)

    pending = [scores(blk) for blk in range(min(SCORES_AHEAD, n_blocks))]
    for blk in range(n_blocks):
        cols = slice(blk * WINDOW, (blk + 1) * WINDOW)
        win = slice(blk * WINDOW, (blk + 2) * WINDOW)
        st = pending.pop(0)
        if blk + SCORES_AHEAD < n_blocks:
            pending.append(scores(blk + SCORES_AHEAD))
        probs = []
        for odd in range(2):
            s_prev = st[:, odd * 2 * WINDOW:odd * 2 * WINDOW + WINDOW]
            s_cur = st[:, odd * 2 * WINDOW + WINDOW:(odd + 1) * 2 * WINDOW]
            sc = jnp.where(from_prev, s_prev, s_cur)
            if blk == 0:
                sc = jnp.where(from_prev & (key_idx < jnp.where(t > 0, -1, WINDOW)), -jnp.inf, sc)
            sink = sink_ref[odd]
            m = jnp.maximum(jnp.max(sc, 1, keepdims=True), sink)
            pr = jnp.exp(sc - m)
            den = jnp.sum(pr, 1, keepdims=True) + jnp.exp(sink - m)
            pr = pr / den
            probs += [jnp.where(from_prev, pr, 0.0).astype(BF16), jnp.where(from_prev, 0.0, pr).astype(BF16)]
        vbd = jnp.stack([jnp.concatenate([vtlo_ref[g * 128:(g + 1) * 128, win],
                                          vthi_ref[g * 128:(g + 1) * 128, win]], 1) for g in groups])
        ot = bdot(vbd, jnp.concatenate(probs, 1))
        for g in groups:
            o = ot[g].T
            out_ref[cols, g * 256:g * 256 + 128] = o[0:WINDOW].astype(out_ref.dtype)
            out_ref[cols, g * 256 + 128:(g + 1) * 256] = o[WINDOW:].astype(out_ref.dtype)

    klo_ref[0:WINDOW, :] = klo_ref[ts:ts + WINDOW, :]
    khi_ref[0:WINDOW, :] = khi_ref[ts:ts + WINDOW, :]
    vtlo_ref[:, 0:WINDOW] = vtlo_ref[:, ts:ts + WINDOW]
    vthi_ref[:, 0:WINDOW] = vthi_ref[:, ts:ts + WINDOW]


def _mixer_c(h, g0, positions, w_qkv, b_qkv, sinks, batch, seq):
    tokens = batch * seq
    ts = min(C_TILE, seq)
    tiles = seq // ts
    half = C_HD // 2
    inv_col = (ROPE_THETA ** (-jnp.arange(half, dtype=F32) / half))[:, None]
    pos = positions.reshape(batch * tiles, 1, ts)
    wt, b_col = w_qkv.T.astype(BF16), b_qkv[:, None]
    w_q, b_q = wt[:C_Q], b_col[:C_Q]
    dup = lambda a: jnp.repeat(a.reshape(C_KV_HEADS, 1, C_HD, -1), 2, 1).reshape(2 * C_KV, -1)
    w_k, b_k = dup(wt[C_Q:C_Q + C_KV]), dup(b_col[C_Q:C_Q + C_KV])
    w_v, b_v = wt[C_Q + C_KV:], b_col[C_Q + C_KV:]
    sink_rows = jnp.repeat(sinks.reshape(C_KV_HEADS, 2, 2).transpose(2, 0, 1), WINDOW, -1)[:, :, None, :]
    const = lambda shape: pl.BlockSpec(shape, lambda b, t: (0,) * len(shape),
                                       pipeline_mode=pl.Buffered(1))
    return pl.pallas_call(
        _mixer_c_kernel,
        grid=(batch, tiles),
        in_specs=[
            pl.BlockSpec((ts, D_MODEL), lambda b, t: (b * tiles + t, 0)),
            pl.BlockSpec((None, 1, ts), lambda b, t: (b * tiles + t, 0, 0)),
            const((half, 1)),
            const((2, C_KV_HEADS, 1, 2 * WINDOW)),
            const((1, D_MODEL)),
            const((C_Q, D_MODEL)),
            const((C_Q, 1)),
            const((2 * C_KV, D_MODEL)),
            const((2 * C_KV, 1)),
            const((C_KV, D_MODEL)),
            const((C_KV, 1)),
        ],
        out_specs=pl.BlockSpec((ts, C_Q), lambda b, t: (b * tiles + t, 0)),
        out_shape=jax.ShapeDtypeStruct((tokens, C_Q), BF16),
        scratch_shapes=[
            pltpu.VMEM((C_Q, ts), BF16),
            pltpu.VMEM((WINDOW + ts, 2 * C_KV), BF16),
            pltpu.VMEM((WINDOW + ts, 2 * C_KV), BF16),
            pltpu.VMEM((2 * C_KV, WINDOW + ts), BF16),
            pltpu.VMEM((2 * C_KV, WINDOW + ts), BF16),
        ],
        compiler_params=pltpu.CompilerParams(
            dimension_semantics=("arbitrary", "arbitrary"),
            vmem_limit_bytes=VMEM_LIMIT_BYTES),
        name="mixer_c",
    )(h, pos, inv_col, sink_rows, g0[None], w_q, b_q, w_k, b_k, w_v, b_v)


def _post_kernel(mix_ref, h_ref, p_ref, wout_ref, bout_ref, gains_ref, wup_ref, wdown_ref,
                 wgate_ref, wple_ref, out_ref):
    mix = jnp.dot(mix_ref[...], wout_ref[...], preferred_element_type=F32) + bout_ref[...]
    h = h_ref[...] + _rms(mix, gains_ref[1:2, :])
    hn = _rms(h, gains_ref[2:3, :]).astype(BF16)
    slab = D_FF // FF_SPLIT
    ff = None
    for j in range(FF_SPLIT):
        up = jnp.dot(hn, wup_ref[:, j * slab:(j + 1) * slab], preferred_element_type=F32)
        act = jnp.square(jnp.maximum(up, 0.0)).astype(BF16)
        part = jnp.dot(act, wdown_ref[j * slab:(j + 1) * slab, :], preferred_element_type=F32)
        ff = part if ff is None else ff + part
    h = h + _rms(ff, gains_ref[3:4, :])
    gate = _sigmoid(jnp.dot(h.astype(BF16), wgate_ref[...], preferred_element_type=F32))
    ple = jnp.dot(p_ref[...].astype(BF16), wple_ref[...], preferred_element_type=F32)
    out_ref[...] = h + gate * ple


def _post(mixed, h, p_all, layer, w_out, b_out, gains, w_up, w_down, w_gate, w_ple):
    tokens = h.shape[0]
    tm = min(POST_TILE, tokens)
    const = lambda shape: pl.BlockSpec(shape, lambda i: (0,) * len(shape),
                                       pipeline_mode=pl.Buffered(1))
    return pl.pallas_call(
        _post_kernel,
        grid=(tokens // tm,),
        in_specs=[
            pl.BlockSpec((tm, D_MODEL), lambda i: (i, 0)),
            pl.BlockSpec((tm, D_MODEL), lambda i: (i, 0)),
            pl.BlockSpec((None, tm, PLE_DIM), lambda i: (layer, i, 0)),
            const((D_MODEL, D_MODEL)),
            const((1, D_MODEL)),
            const((4, D_MODEL)),
            const((D_MODEL, D_FF)),
            const((D_FF, D_MODEL)),
            const((D_MODEL, D_MODEL)),
            const((PLE_DIM, D_MODEL)),
        ],
        out_specs=pl.BlockSpec((tm, D_MODEL), lambda i: (i, 0)),
        out_shape=jax.ShapeDtypeStruct((tokens, D_MODEL), F32),
        compiler_params=pltpu.CompilerParams(
            dimension_semantics=("arbitrary",),
            vmem_limit_bytes=VMEM_LIMIT_BYTES),
        name="post",
    )(mixed, h, p_all, w_out.astype(BF16), b_out[None], gains, w_up.astype(BF16),
      w_down.astype(BF16), w_gate.astype(BF16), w_ple.astype(BF16))


def kernel(x, p, positions, norm_gains, w_in_ab, conv_a, a_log, dt_bias, norm_a, i_bias_b, f_bias_b, norm_b, w_out_ab, w_qkv_c, b_qkv_c, sinks_c, w_o_c, b_o_c, w_up, w_down, w_ple, w_ple_gate):
    batch, seq, _ = x.shape
    depth = norm_gains.shape[0]
    tokens = batch * seq
    h = x.reshape(tokens, D_MODEL)
    p_all = p.reshape(depth, tokens, PLE_DIM)
    for layer in range(depth):
        gains = norm_gains[layer]
        if layer % 2 == 0:
            e = layer // 2
            mixed = _mixer_ab(h, gains[0], w_in_ab[e], conv_a[e], a_log[e], dt_bias[e], norm_a[e],
                              i_bias_b[e], f_bias_b[e], norm_b[e], batch, seq)
            w_out, b_out = w_out_ab[e], jnp.zeros((D_MODEL,), F32)
        else:
            o = layer // 2
            mixed = _mixer_c(h, gains[0], positions, w_qkv_c[o], b_qkv_c[o], sinks_c[o], batch, seq)
            w_out, b_out = w_o_c[o], b_o_c[o]
        h = _post(mixed, h, p_all, layer, w_out, b_out, gains, w_up[layer], w_down[layer],
                  w_ple_gate[layer], w_ple[layer])
    return h.reshape(batch, seq, D_MODEL)
```

```python
import functools

import jax
import jax.numpy as jnp
from jax import lax
from jax.experimental import pallas as pl
from jax.experimental.pallas import tpu as pltpu

F32 = jnp.float32
BF16 = jnp.bfloat16
HIGHEST = lax.Precision.HIGHEST

EPS = 1e-6
D_MODEL = 1024
HEADS_AB = 4
HEAD_DIM_AB = 128
CONV_K = 4
CHUNK = 64
C_HEADS = 16
C_KV_HEADS = 4
C_GROUP = C_HEADS // C_KV_HEADS
C_HD = 64
WINDOW = 128
ROPE_THETA = 10000.0
D_FF = 4 * D_MODEL
PLE_DIM = 256
QK_AB = HEADS_AB * HEAD_DIM_AB
CONV_CH = 3 * QK_AB
BIG_COLS = 8 * QK_AB
GATE_COLS = 128
C_Q = C_HEADS * C_HD
C_KV = C_KV_HEADS * C_HD

AB_ROWS = 2
AB_TILE = 256
C_TILE = 2048
POST_TILE = 1024
PREP_CHUNKS = 8
SCORES_AHEAD = 4
SEQ_UNROLL = 4
FF_SPLIT = 8
VMEM_LIMIT_BYTES = 56 * 1024 * 1024


def _rms(x, g):
    return x * lax.rsqrt(jnp.mean(x * x, -1, keepdims=True) + EPS) * g


def _dot(a, b):
    return jnp.dot(a.astype(BF16), b.astype(BF16), preferred_element_type=F32)


def _dot_nt(a, b):
    return lax.dot_general(a.astype(BF16), b.astype(BF16), (((1,), (1,)), ((), ())),
                           preferred_element_type=F32)


def _dot_tn(a, b):
    return lax.dot_general(a.astype(BF16), b.astype(BF16), (((0,), (0,)), ((), ())),
                           preferred_element_type=F32)


def _dot_hi(a, b):
    return jnp.dot(a, b, preferred_element_type=F32, precision=HIGHEST)


def _split(a):
    hi = a.astype(BF16)
    hi_f = hi.astype(F32)
    lo_f = a - hi_f
    left = lax.broadcasted_iota(jnp.int32, a.shape, 2) < a.shape[2] // 2
    return hi, lo_f.astype(BF16), jnp.where(left, hi_f, lo_f).astype(BF16)


def _dot3(a, b):
    (ah, _, ahl), (bh, bl, _) = a, b
    d = lambda x, y: lax.dot_general(x, y, (((2,), (1,)), ((0,), (0,))), preferred_element_type=F32)
    return d(ahl, jnp.concatenate([bl, bh], 1)) + d(ah[:, :, 0:ah.shape[2] // 2], bh)


def _softplus(x):
    return jnp.maximum(x, 0.0) + jnp.log1p(jnp.exp(-jnp.abs(x)))


def _sigmoid(x):
    return 1.0 / (1.0 + jnp.exp(-x))


def _silu(x):
    return x * _sigmoid(x)


def _mixer_ab_kernel(h_ref, g0_ref, wbig_ref, wgate_ref, convw_ref, gprm_ref, norma_ref,
                     normb_ref, out_ref,
                     proj_ref, xbuf_ref, qkv_ref, gate_ref, gcum_ref, gt_ref, gcumt_ref, gcb_ref, bcb_ref,
                     u_ref, wq_ref, attn_ref, kdect_ref, eg_ref,
                     qbb_ref, pv0_ref, rs0_ref, mib_ref, kn0_ref, bl_ref, wm_ref,
                     sa_ref, cn_ref, mb_ref):
    n_rows, tile = h_ref.shape[0], h_ref.shape[1]
    ts = n_rows * tile
    n_chunks = ts // CHUNK
    chunks_per_row = tile // CHUNK
    t = pl.program_id(1)

    @pl.when(t == 0)
    def _():
        xbuf_ref[:, 0:8, :] = jnp.zeros((n_rows, 8, CONV_CH), F32)
        sa_ref[...] = jnp.zeros_like(sa_ref)
        cn_ref[...] = jnp.zeros_like(cn_ref)
        mb_ref[...] = jnp.zeros_like(mb_ref)

    hn = _rms(h_ref[...].reshape(ts, D_MODEL), g0_ref[...]).astype(BF16)
    gates_pre = jnp.dot(hn, wgate_ref[...], preferred_element_type=F32) + gprm_ref[0:1, :]

    lane = lax.broadcasted_iota(jnp.int32, (1, GATE_COLS), 1)
    neg_a = -jnp.exp(gprm_ref[1:2, :])
    gate = jnp.where(lane < 4, _sigmoid(gates_pre),
                     jnp.where(lane < 8, neg_a * _softplus(gates_pre),
                               jnp.where(lane < 12, gates_pre, -_softplus(-gates_pre))))
    gate_ref[...] = gate
    row = lax.broadcasted_iota(jnp.int32, (CHUNK, CHUNK), 0)
    col = lax.broadcasted_iota(jnp.int32, (CHUNK, CHUNK), 1)
    causal = row >= col
    tri = causal.astype(F32)
    row2 = lax.broadcasted_iota(jnp.int32, (CHUNK, 2 * CHUNK), 0)
    col2 = lax.broadcasted_iota(jnp.int32, (CHUNK, 2 * CHUNK), 1) % CHUNK
    causal2 = row2 >= col2
    strict2 = row2 > col2
    eye2 = (row2 == col2).astype(F32)
    twice = lambda x: jnp.concatenate([x, x], 1)
    for c in range(n_chunks):
        gch = gate_ref[c * CHUNK:(c + 1) * CHUNK, :]
        cum = _dot_hi(tri, gch)
        gcum_ref[c * CHUNK:(c + 1) * CHUNK, :] = cum
        gt_ref[c] = twice(gch.T[0:16, :])
        gcumt_ref[c] = twice(cum.T[0:16, :])
    for hd in range(HEADS_AB):
        gcb_ref[hd] = jnp.broadcast_to(gcum_ref[:, 4 + hd:5 + hd], (ts, HEAD_DIM_AB))
        bcb_ref[hd] = jnp.broadcast_to(gcum_ref[:, 12 + hd:13 + hd], (ts, HEAD_DIM_AB))

    proj_ref[...] = jnp.dot(hn, wbig_ref[...], preferred_element_type=F32)
    proj_ref[:, 3 * QK_AB:4 * QK_AB] = _silu(proj_ref[:, 3 * QK_AB:4 * QK_AB])
    proj_ref[:, 7 * QK_AB:8 * QK_AB] = _sigmoid(proj_ref[:, 7 * QK_AB:8 * QK_AB])

    for r in range(n_rows):
        xbuf_ref[r, 8:8 + tile, :] = proj_ref[r * tile:(r + 1) * tile, 0:CONV_CH]
        xfull = xbuf_ref[r]
        conv = convw_ref[0:1, :] * xfull
        for k in range(1, CONV_K):
            conv = convw_ref[k:k + 1, :] * xfull + pltpu.roll(conv, 1, 0)
        xbuf_ref[r, 0:8, :] = xbuf_ref[r, tile:tile + 8, :]
        qkv_ref[r * tile:(r + 1) * tile, :] = _silu(conv[8:, :])
    for hd in range(HEADS_AB):
        sl = slice(hd * HEAD_DIM_AB, (hd + 1) * HEAD_DIM_AB)
        q = qkv_ref[:, sl]
        qkv_ref[:, sl] = q * lax.rsqrt(jnp.sum(q * q, -1, keepdims=True) + EPS) * (HEAD_DIM_AB ** -0.5)
        sl = slice(QK_AB + hd * HEAD_DIM_AB, QK_AB + (hd + 1) * HEAD_DIM_AB)
        k = qkv_ref[:, sl]
        qkv_ref[:, sl] = k * lax.rsqrt(jnp.sum(k * k, -1, keepdims=True) + EPS)

    norm_a = norma_ref[...]

    def prep_body(i, carry):
        lmats, rhs = [], []
        for j in range(PREP_CHUNKS):
            c = i * PREP_CHUNKS + j
            r0 = pl.multiple_of(c * CHUNK, CHUNK)
            rows = pl.ds(r0, CHUNK)
            gct = gcumt_ref[c]
            for hd in range(HEADS_AB):
                lo, hi = hd * HEAD_DIM_AB, (hd + 1) * HEAD_DIM_AB
                q = qkv_ref[rows, lo:hi]
                k = qkv_ref[rows, QK_AB + lo:QK_AB + hi]
                v = qkv_ref[rows, 2 * QK_AB + lo:2 * QK_AB + hi]
                gtc = gt_ref[c]
                beta_row = gtc[hd:hd + 1, :]
                gc_row = gct[4 + hd:5 + hd, :]
                gc_all = gcb_ref[hd, rows, :]
                g_last = gc_all[CHUNK - 1:CHUNK, :]
                diff = gc_all - gc_row
                decay = jnp.where(causal2, jnp.exp(jnp.where(causal2, diff, 0.0)), 0.0)
                egc = jnp.exp(gc_all)
                k_bf = k.astype(BF16)
                kk = _dot_nt(k_bf, jnp.concatenate([k_bf, k_bf], 0))
                lmats.append(jnp.where(strict2, kk * decay, 0.0) * beta_row)
                rhs.append((c, hd, v.astype(BF16), (k * egc).astype(BF16)))
                wq_ref[c, hd, CHUNK:2 * CHUNK, :] = (q * egc).astype(BF16)
                attn_ref[c, hd] = (_dot_nt(q, k_bf) * (decay * beta_row)[:, 0:CHUNK]).astype(BF16)
                kdect_ref[c, hd] = ((k * jnp.exp(g_last - gc_all)).T * beta_row[:, 0:CHUNK]).astype(BF16)
                eg_ref[c, hd] = jnp.exp(g_last)
                qb = proj_ref[rows, 4 * QK_AB + lo:4 * QK_AB + hi] * (HEAD_DIM_AB ** -0.5)
                kbb = proj_ref[rows, 5 * QK_AB + lo:5 * QK_AB + hi]
                vbb = proj_ref[rows, 6 * QK_AB + lo:6 * QK_AB + hi].astype(BF16)
                bc_all = bcb_ref[hd, rows, :]
                bc_row = gct[12 + hd:13 + hd, 0:CHUNK]
                i_row = gtc[8 + hd:9 + hd, 0:CHUNK]
                b_last = bc_all[CHUNK - 1:CHUNK, :]
                dmat = jnp.where(causal, bc_all[:, 0:CHUNK] - bc_row + i_row, -jnp.inf)
                m_intra = jnp.max(dmat, -1, keepdims=True)
                p0 = _dot_nt(qb, kbb) * jnp.exp(dmat - m_intra)
                qbb_ref[c, hd] = qb.astype(BF16)
                pv0_ref[c, hd] = _dot(p0, vbb)
                rs0_ref[c, hd] = jnp.broadcast_to(jnp.sum(p0, -1, keepdims=True), (CHUNK, HEAD_DIM_AB))
                mib_ref[c, hd] = jnp.broadcast_to(m_intra, (CHUNK, HEAD_DIM_AB))
                ws_row = b_last[:, 0:CHUNK] - bc_row + i_row
                ws_max = jnp.max(ws_row, -1, keepdims=True)
                kw_t = kbb.T * jnp.exp(ws_row - ws_max)
                kn0_ref[c, hd, :, 0:HEAD_DIM_AB] = _dot(kw_t, vbb)
                kn0_ref[c, hd, :, HEAD_DIM_AB:] = jnp.broadcast_to(
                    jnp.sum(kw_t, -1, keepdims=True), (HEAD_DIM_AB, HEAD_DIM_AB))
                bl_ref[c, hd] = jnp.concatenate([b_last, b_last], 1)
                wm_ref[c, hd] = jnp.broadcast_to(ws_max, (1, 2 * HEAD_DIM_AB))
        lmat = jnp.stack(lmats)
        l_split = _split(lmat)
        tinv = eye2 - lmat
        pw = _dot3(l_split, l_split)
        for _ in range(4):
            pw_split = _split(pw)
            tinv = tinv + _dot3(_split(tinv), pw_split)
            pw = _dot3(pw_split, pw_split)
        tinv = (tinv + _dot3(_split(tinv), _split(pw)))[:, :, 0:CHUNK].astype(BF16)
        for n, (c, hd, v_bf, ke_bf) in enumerate(rhs):
            u_ref[c, hd] = jnp.dot(tinv[n], v_bf, preferred_element_type=F32)
            wq_ref[c, hd, 0:CHUNK, :] = jnp.dot(tinv[n], ke_bf, preferred_element_type=F32).astype(BF16)
        return carry

    lax.fori_loop(0, n_chunks // PREP_CHUNKS, prep_body, 0)

    def chunk_body(j, carry):
        local = pl.ds(pl.multiple_of(j * CHUNK, CHUNK), CHUNK)
        rows = [pl.ds(pl.multiple_of(r * tile + j * CHUNK, CHUNK), CHUNK) for r in range(n_rows)]
        both = lambda ref: jnp.concatenate([ref[r * chunks_per_row + j] for r in range(n_rows)], 0)
        bdot = lambda x, y: lax.dot_general(x, y, (((2,), (1,)), ((0,), (0,))),
                                            preferred_element_type=F32)
        state = sa_ref[...]
        ws_qs = bdot(both(wq_ref), state.astype(BF16))
        cn = cn_ref[...]
        q_cn = bdot(both(qbb_ref), cn.astype(BF16))

        v_new = (both(u_ref) - ws_qs[:, 0:CHUNK]).astype(BF16)
        o = ws_qs[:, CHUNK:] + bdot(both(attn_ref), v_new)
        sa_ref[...] = state * both(eg_ref) + bdot(both(kdect_ref), v_new)

        m_prev = mb_ref[:, 0:1, :]
        a = jnp.concatenate([bcb_ref[:, rw, :] for rw in rows], 0) + m_prev[:, :, 0:HEAD_DIM_AB]
        m_intra = both(mib_ref)
        mt = jnp.maximum(a, m_intra)
        inter = jnp.exp(a - mt)
        intra = jnp.exp(m_intra - mt)
        num = inter * q_cn[:, :, 0:HEAD_DIM_AB] + intra * both(pv0_ref)
        den = inter * q_cn[:, :, HEAD_DIM_AB:] + intra * both(rs0_ref)
        hout = num / jnp.maximum(jnp.abs(den), jnp.exp(-mt))
        b_last, ws_max = both(bl_ref), both(wm_ref)
        m_new = jnp.maximum(b_last + m_prev, ws_max)
        cn_ref[...] = jnp.exp(b_last + m_prev - m_new) * cn + jnp.exp(ws_max - m_new) * both(kn0_ref)
        mb_ref[...] = jnp.broadcast_to(m_new, mb_ref.shape)

        for r in range(n_rows):
            for hd in range(HEADS_AB):
                n = r * HEADS_AB + hd
                lo, hi = hd * HEAD_DIM_AB, (hd + 1) * HEAD_DIM_AB
                z_gate = proj_ref[rows[r], 3 * QK_AB + lo:3 * QK_AB + hi]
                out_ref[r, local, lo:hi] = (_rms(o[n], norm_a) * z_gate).astype(out_ref.dtype)
                o_gate = proj_ref[rows[r], 7 * QK_AB + lo:7 * QK_AB + hi]
                out_ref[r, local, QK_AB + lo:QK_AB + hi] = (
                    _rms(hout[n], normb_ref[hd:hd + 1, :]) * o_gate).astype(out_ref.dtype)
        return carry

    lax.fori_loop(0, chunks_per_row, chunk_body, 0, unroll=SEQ_UNROLL)


def _mixer_ab(h, g0, w_in, conv_w, a_log, dt_bias, norm_a, i_bias, f_bias, norm_b, batch, seq):
    tile = min(AB_TILE, seq)
    n_rows = AB_ROWS if batch % AB_ROWS == 0 else 1
    tiles = seq // tile
    ts = n_rows * tile
    a_cols = 2 * QK_AB + QK_AB + 2 * HEADS_AB + QK_AB
    wa, wb = w_in[:, :a_cols], w_in[:, a_cols:]
    g_off = 3 * QK_AB
    z_off = g_off + 2 * HEADS_AB
    w_big = jnp.concatenate([wa[:, :g_off], wa[:, z_off:], wb[:, :g_off], wb[:, z_off:]], 1).astype(BF16)
    w_gate = jnp.concatenate([wa[:, g_off:z_off], wb[:, g_off:z_off],
                              jnp.zeros((D_MODEL, GATE_COLS - 4 * HEADS_AB), F32)], 1).astype(BF16)
    zeros4 = jnp.zeros((HEADS_AB,), F32)
    pad = jnp.zeros((GATE_COLS - 4 * HEADS_AB,), F32)
    bias_row = jnp.concatenate([zeros4, dt_bias, i_bias, f_bias, pad])
    alog_row = jnp.concatenate([zeros4, a_log, zeros4, zeros4, pad])
    gprm = jnp.concatenate([bias_row[None], alog_row[None], jnp.zeros((6, GATE_COLS), F32)], 0)

    const = lambda shape: pl.BlockSpec(shape, lambda b, t: (0,) * len(shape),
                                       pipeline_mode=pl.Buffered(1))
    n_chunks = ts // CHUNK
    per_head = (n_chunks, HEADS_AB)
    all_heads = n_rows * HEADS_AB
    return pl.pallas_call(
        _mixer_ab_kernel,
        grid=(batch // n_rows, tiles),
        in_specs=[
            pl.BlockSpec((n_rows, tile, D_MODEL), lambda b, t: (b, t, 0)),
            const((1, D_MODEL)),
            const((D_MODEL, BIG_COLS)),
            const((D_MODEL, GATE_COLS)),
            const((CONV_K, CONV_CH)),
            const((8, GATE_COLS)),
            const((1, HEAD_DIM_AB)),
            const((HEADS_AB, HEAD_DIM_AB)),
        ],
        out_specs=pl.BlockSpec((n_rows, tile, 2 * QK_AB), lambda b, t: (b, t, 0)),
        out_shape=jax.ShapeDtypeStruct((batch, seq, 2 * QK_AB), BF16),
        scratch_shapes=[
            pltpu.VMEM((ts, BIG_COLS), F32),
            pltpu.VMEM((n_rows, tile + 8, CONV_CH), F32),
            pltpu.VMEM((ts, CONV_CH), F32),
            pltpu.VMEM((ts, GATE_COLS), F32),
            pltpu.VMEM((ts, GATE_COLS), F32),
            pltpu.VMEM((n_chunks, 16, 2 * CHUNK), F32),
            pltpu.VMEM((n_chunks, 16, 2 * CHUNK), F32),
            pltpu.VMEM((HEADS_AB, ts, HEAD_DIM_AB), F32),
            pltpu.VMEM((HEADS_AB, ts, HEAD_DIM_AB), F32),
            pltpu.VMEM(per_head + (CHUNK, HEAD_DIM_AB), F32),
            pltpu.VMEM(per_head + (2 * CHUNK, HEAD_DIM_AB), BF16),
            pltpu.VMEM(per_head + (CHUNK, CHUNK), BF16),
            pltpu.VMEM(per_head + (HEAD_DIM_AB, CHUNK), BF16),
            pltpu.VMEM(per_head + (1, HEAD_DIM_AB), F32),
            pltpu.VMEM(per_head + (CHUNK, HEAD_DIM_AB), BF16),
            pltpu.VMEM(per_head + (CHUNK, HEAD_DIM_AB), F32),
            pltpu.VMEM(per_head + (CHUNK, HEAD_DIM_AB), F32),
            pltpu.VMEM(per_head + (CHUNK, HEAD_DIM_AB), F32),
            pltpu.VMEM(per_head + (HEAD_DIM_AB, 2 * HEAD_DIM_AB), F32),
            pltpu.VMEM(per_head + (1, 2 * HEAD_DIM_AB), F32),
            pltpu.VMEM(per_head + (1, 2 * HEAD_DIM_AB), F32),
            pltpu.VMEM((all_heads, HEAD_DIM_AB, HEAD_DIM_AB), F32),
            pltpu.VMEM((all_heads, HEAD_DIM_AB, 2 * HEAD_DIM_AB), F32),
            pltpu.VMEM((all_heads, 8, 2 * HEAD_DIM_AB), F32),
        ],
        compiler_params=pltpu.CompilerParams(
            dimension_semantics=("arbitrary", "arbitrary"),
            vmem_limit_bytes=VMEM_LIMIT_BYTES),
        name="mixer_ab",
    )(h.reshape(batch, seq, D_MODEL), g0[None], w_big, w_gate, conv_w, gprm, norm_a[None],
      norm_b).reshape(batch * seq, 2 * QK_AB)


def _mixer_c_kernel(h_ref, pos_ref, invc_ref, sink_ref, g0_ref, wqt_ref, bq_ref, wkt_ref, bk_ref,
                    wvt_ref, bv_ref, out_ref, q_ref, klo_ref, khi_ref, vtlo_ref, vthi_ref):
    ts = h_ref.shape[0]
    n_blocks = ts // WINDOW
    half = C_HD // 2
    t = pl.program_id(1)

    @pl.when(t == 0)
    def _():
        vtlo_ref[...] = jnp.zeros_like(vtlo_ref)
        vthi_ref[...] = jnp.zeros_like(vthi_ref)
        klo_ref[0:WINDOW, :] = jnp.zeros((WINDOW, klo_ref.shape[1]), BF16)
        khi_ref[0:WINDOW, :] = jnp.zeros((WINDOW, khi_ref.shape[1]), BF16)

    hn = _rms(h_ref[...], g0_ref[...]).astype(BF16)

    ang_t = pos_ref[...].astype(F32) * invc_ref[...]
    cos_t = jnp.cos(ang_t)
    sin_t = jnp.sin(ang_t)

    def rope_t(x):
        x1, x2 = x[0:half], x[half:C_HD]
        return jnp.concatenate([x1 * cos_t - x2 * sin_t, x2 * cos_t + x1 * sin_t], 0)

    qt = _dot_nt(wqt_ref[...], hn) + bq_ref[...]
    for hd in range(C_HEADS):
        rows = slice(hd * C_HD, (hd + 1) * C_HD)
        q_ref[rows, :] = (rope_t(qt[rows]) * (C_HD ** -0.5)).astype(BF16)

    kt = _dot_nt(wkt_ref[...], hn) + bk_ref[...]
    kk = jnp.concatenate([rope_t(kt[j * C_HD:(j + 1) * C_HD]) for j in range(2 * C_KV_HEADS)], 0).T
    low = (lax.broadcasted_iota(jnp.int32, (1, 2 * C_KV), 1) % 128) < C_HD
    klo_ref[WINDOW:, :] = jnp.where(low, kk, 0.0).astype(BF16)
    khi_ref[WINDOW:, :] = jnp.where(low, 0.0, kk).astype(BF16)

    vt = (_dot_nt(wvt_ref[...], hn) + bv_ref[...]).astype(BF16)
    for g in range(C_KV_HEADS):
        vtlo_ref[2 * g * C_HD:(2 * g + 1) * C_HD, WINDOW:] = vt[g * C_HD:(g + 1) * C_HD]
        vthi_ref[(2 * g + 1) * C_HD:(2 * g + 2) * C_HD, WINDOW:] = vt[g * C_HD:(g + 1) * C_HD]

    bdot = lambda x, y: lax.dot_general(x, y, (((2,), (1,)), ((0,), (0,))), preferred_element_type=F32)
    groups = range(C_KV_HEADS)
    key_idx = lax.broadcasted_iota(jnp.int32, (WINDOW, 2 * WINDOW), 0)
    qry_idx = lax.broadcasted_iota(jnp.int32, (WINDOW, 2 * WINDOW), 1) % WINDOW
    from_prev = key_idx > qry_idx
    def scores(blk):
        cols = slice(blk * WINDOW, (blk + 1) * WINDOW)
        win = slice(blk * WINDOW, (blk + 2) * WINDOW)
        kbd = jnp.stack([jnp.concatenate([klo_ref[win, g * 128:(g + 1) * 128],
                                          khi_ref[win, g * 128:(g + 1) * 128]], 0) for g in groups])
        qbd = jnp.stack([jnp.concatenate([q_ref[g * 256:g * 256 + 128, cols],
                                          q_ref[g * 256 + 128:(g + 1) * 256, cols]], 1) for g in groups])
        return bdot(kbd, qbd)

    pending = [scores(blk) for blk in range(min(SCORES_AHEAD, n_blocks))]
    for blk in range(n_blocks):
        cols = slice(blk * WINDOW, (blk + 1) * WINDOW)
        win = slice(blk * WINDOW, (blk + 2) * WINDOW)
        st = pending.pop(0)
        if blk + SCORES_AHEAD < n_blocks:
            pending.append(scores(blk + SCORES_AHEAD))
        probs = []
        for odd in range(2):
            s_prev = st[:, odd * 2 * WINDOW:odd * 2 * WINDOW + WINDOW]
            s_cur = st[:, odd * 2 * WINDOW + WINDOW:(odd + 1) * 2 * WINDOW]
            sc = jnp.where(from_prev, s_prev, s_cur)
            if blk == 0:
                sc = jnp.where(from_prev & (key_idx < jnp.where(t > 0, -1, WINDOW)), -jnp.inf, sc)
            sink = sink_ref[odd]
            m = jnp.maximum(jnp.max(sc, 1, keepdims=True), sink)
            pr = jnp.exp(sc - m)
            den = jnp.sum(pr, 1, keepdims=True) + jnp.exp(sink - m)
            pr = pr / den
            probs += [jnp.where(from_prev, pr, 0.0).astype(BF16), jnp.where(from_prev, 0.0, pr).astype(BF16)]
        vbd = jnp.stack([jnp.concatenate([vtlo_ref[g * 128:(g + 1) * 128, win],
                                          vthi_ref[g * 128:(g + 1) * 128, win]], 1) for g in groups])
        ot = bdot(vbd, jnp.concatenate(probs, 1))
        for g in groups:
            o = ot[g].T
            out_ref[cols, g * 256:g * 256 + 128] = o[0:WINDOW].astype(out_ref.dtype)
            out_ref[cols, g * 256 + 128:(g + 1) * 256] = o[WINDOW:].astype(out_ref.dtype)

    klo_ref[0:WINDOW, :] = klo_ref[ts:ts + WINDOW, :]
    khi_ref[0:WINDOW, :] = khi_ref[ts:ts + WINDOW, :]
    vtlo_ref[:, 0:WINDOW] = vtlo_ref[:, ts:ts + WINDOW]
    vthi_ref[:, 0:WINDOW] = vthi_ref[:, ts:ts + WINDOW]


def _mixer_c(h, g0, positions, w_qkv, b_qkv, sinks, batch, seq):
    tokens = batch * seq
    ts = min(C_TILE, seq)
    tiles = seq // ts
    half = C_HD // 2
    inv_col = (ROPE_THETA ** (-jnp.arange(half, dtype=F32) / half))[:, None]
    pos = positions.reshape(batch * tiles, 1, ts)
    wt, b_col = w_qkv.T.astype(BF16), b_qkv[:, None]
    w_q, b_q = wt[:C_Q], b_col[:C_Q]
    dup = lambda a: jnp.repeat(a.reshape(C_KV_HEADS, 1, C_HD, -1), 2, 1).reshape(2 * C_KV, -1)
    w_k, b_k = dup(wt[C_Q:C_Q + C_KV]), dup(b_col[C_Q:C_Q + C_KV])
    w_v, b_v = wt[C_Q + C_KV:], b_col[C_Q + C_KV:]
    sink_rows = jnp.repeat(sinks.reshape(C_KV_HEADS, 2, 2).transpose(2, 0, 1), WINDOW, -1)[:, :, None, :]
    const = lambda shape: pl.BlockSpec(shape, lambda b, t: (0,) * len(shape),
                                       pipeline_mode=pl.Buffered(1))
    return pl.pallas_call(
        _mixer_c_kernel,
        grid=(batch, tiles),
        in_specs=[
            pl.BlockSpec((ts, D_MODEL), lambda b, t: (b * tiles + t, 0)),
            pl.BlockSpec((None, 1, ts), lambda b, t: (b * tiles + t, 0, 0)),
            const((half, 1)),
            const((2, C_KV_HEADS, 1, 2 * WINDOW)),
            const((1, D_MODEL)),
            const((C_Q, D_MODEL)),
            const((C_Q, 1)),
            const((2 * C_KV, D_MODEL)),
            const((2 * C_KV, 1)),
            const((C_KV, D_MODEL)),
            const((C_KV, 1)),
        ],
        out_specs=pl.BlockSpec((ts, C_Q), lambda b, t: (b * tiles + t, 0)),
        out_shape=jax.ShapeDtypeStruct((tokens, C_Q), BF16),
        scratch_shapes=[
            pltpu.VMEM((C_Q, ts), BF16),
            pltpu.VMEM((WINDOW + ts, 2 * C_KV), BF16),
            pltpu.VMEM((WINDOW + ts, 2 * C_KV), BF16),
            pltpu.VMEM((2 * C_KV, WINDOW + ts), BF16),
            pltpu.VMEM((2 * C_KV, WINDOW + ts), BF16),
        ],
        compiler_params=pltpu.CompilerParams(
            dimension_semantics=("arbitrary", "arbitrary"),
            vmem_limit_bytes=VMEM_LIMIT_BYTES),
        name="mixer_c",
    )(h, pos, inv_col, sink_rows, g0[None], w_q, b_q, w_k, b_k, w_v, b_v)


def _post_kernel(mix_ref, h_ref, p_ref, wout_ref, bout_ref, gains_ref, wup_ref, wdown_ref,
                 wgate_ref, wple_ref, out_ref):
    mix = jnp.dot(mix_ref[...], wout_ref[...], preferred_element_type=F32) + bout_ref[...]
    h = h_ref[...] + _rms(mix, gains_ref[1:2, :])
    hn = _rms(h, gains_ref[2:3, :]).astype(BF16)
    slab = D_FF // FF_SPLIT
    ff = None
    for j in range(FF_SPLIT):
        up = jnp.dot(hn, wup_ref[:, j * slab:(j + 1) * slab], preferred_element_type=F32)
        act = jnp.square(jnp.maximum(up, 0.0)).astype(BF16)
        part = jnp.dot(act, wdown_ref[j * slab:(j + 1) * slab, :], preferred_element_type=F32)
        ff = part if ff is None else ff + part
    h = h + _rms(ff, gains_ref[3:4, :])
    gate = _sigmoid(jnp.dot(h.astype(BF16), wgate_ref[...], preferred_element_type=F32))
    ple = jnp.dot(p_ref[...].astype(BF16), wple_ref[...], preferred_element_type=F32)
    out_ref[...] = h + gate * ple


def _post(mixed, h, p_all, layer, w_out, b_out, gains, w_up, w_down, w_gate, w_ple):
    tokens = h.shape[0]
    tm = min(POST_TILE, tokens)
    const = lambda shape: pl.BlockSpec(shape, lambda i: (0,) * len(shape),
                                       pipeline_mode=pl.Buffered(1))
    return pl.pallas_call(
        _post_kernel,
        grid=(tokens // tm,),
        in_specs=[
            pl.BlockSpec((tm, D_MODEL), lambda i: (i, 0)),
            pl.BlockSpec((tm, D_MODEL), lambda i: (i, 0)),
            pl.BlockSpec((None, tm, PLE_DIM), lambda i: (layer, i, 0)),
            const((D_MODEL, D_MODEL)),
            const((1, D_MODEL)),
            const((4, D_MODEL)),
            const((D_MODEL, D_FF)),
            const((D_FF, D_MODEL)),
            const((D_MODEL, D_MODEL)),
            const((PLE_DIM, D_MODEL)),
        ],
        out_specs=pl.BlockSpec((tm, D_MODEL), lambda i: (i, 0)),
        out_shape=jax.ShapeDtypeStruct((tokens, D_MODEL), F32),
        compiler_params=pltpu.CompilerParams(
            dimension_semantics=("arbitrary",),
            vmem_limit_bytes=VMEM_LIMIT_BYTES),
        name="post",
    )(mixed, h, p_all, w_out.astype(BF16), b_out[None], gains, w_up.astype(BF16),
      w_down.astype(BF16), w_gate.astype(BF16), w_ple.astype(BF16))


def kernel(x, p, positions, norm_gains, w_in_ab, conv_a, a_log, dt_bias, norm_a, i_bias_b, f_bias_b, norm_b, w_out_ab, w_qkv_c, b_qkv_c, sinks_c, w_o_c, b_o_c, w_up, w_down, w_ple, w_ple_gate):
    batch, seq, _ = x.shape
    depth = norm_gains.shape[0]
    tokens = batch * seq
    h = x.reshape(tokens, D_MODEL)
    p_all = p.reshape(depth, tokens, PLE_DIM)
    for layer in range(depth):
        gains = norm_gains[layer]
        if layer % 2 == 0:
            e = layer // 2
            mixed = _mixer_ab(h, gains[0], w_in_ab[e], conv_a[e], a_log[e], dt_bias[e], norm_a[e],
                              i_bias_b[e], f_bias_b[e], norm_b[e], batch, seq)
            w_out, b_out = w_out_ab[e], jnp.zeros((D_MODEL,), F32)
        else:
            o = layer // 2
            mixed = _mixer_c(h, gains[0], positions, w_qkv_c[o], b_qkv_c[o], sinks_c[o], batch, seq)
            w_out, b_out = w_o_c[o], b_o_c[o]
        h = _post(mixed, h, p_all, layer, w_out, b_out, gains, w_up[layer], w_down[layer],
                  w_ple_gate[layer], w_ple[layer])
    return h.reshape(batch, seq, D_MODEL)
```

```python
import jax
import jax.numpy as jnp
from jax import lax
from jax.experimental import pallas as pl
from jax.experimental.pallas import tpu as pltpu

F32 = jnp.float32
BF16 = jnp.bfloat16
HIGHEST = lax.Precision.HIGHEST

EPS = 1e-6
D_MODEL = 1024
HEADS_AB = 4
HEAD_DIM_AB = 128
CONV_K = 4
CHUNK = 64
C_HEADS = 16
C_KV_HEADS = 4
C_HD = 64
WINDOW = 128
ROPE_THETA = 10000.0
D_FF = 4 * D_MODEL
PLE_DIM = 256
QK_AB = HEADS_AB * HEAD_DIM_AB
CONV_CH = 3 * QK_AB
BIG_COLS = 8 * QK_AB
GATE_COLS = 128
C_Q = C_HEADS * C_HD
C_KV = C_KV_HEADS * C_HD

AB_ROWS = 2
AB_TILE = 256
C_TILE = 2048
POST_TILE = 1024
PREP_CHUNKS = 8
SCORES_AHEAD = 4
SEQ_UNROLL = 4
FF_SPLIT = 8
VMEM_LIMIT_BYTES = 56 * 1024 * 1024


def _rms(x, g):
    return x * lax.rsqrt(jnp.mean(x * x, -1, keepdims=True) + EPS) * g


def _dot(a, b):
    return jnp.dot(a.astype(BF16), b.astype(BF16), preferred_element_type=F32)


def _dot_nt(a, b):
    return lax.dot_general(a.astype(BF16), b.astype(BF16), (((1,), (1,)), ((), ())),
                           preferred_element_type=F32)


def _dot_hi(a, b):
    return jnp.dot(a, b, preferred_element_type=F32, precision=HIGHEST)


def _split(a):
    hi = a.astype(BF16)
    hi_f = hi.astype(F32)
    lo_f = a - hi_f
    left = lax.broadcasted_iota(jnp.int32, a.shape, 2) < a.shape[2] // 2
    return hi, lo_f.astype(BF16), jnp.where(left, hi_f, lo_f).astype(BF16)


def _dot3(a, b):
    (ah, _, ahl), (bh, bl, _) = a, b
    lhs = jnp.concatenate([ahl, ah], 2)
    rhs = jnp.concatenate([bl, bh, bh, jnp.zeros_like(bh)], 1)
    return lax.dot_general(lhs, rhs, (((2,), (1,)), ((0,), (0,))), preferred_element_type=F32)


def _softplus(x):
    return jnp.maximum(x, 0.0) + jnp.log1p(jnp.exp(-jnp.abs(x)))


def _sigmoid(x):
    return 1.0 / (1.0 + jnp.exp(-x))


def _silu(x):
    return x * _sigmoid(x)


def _mixer_ab_kernel(h_ref, g0_ref, wbig_ref, wgate_ref, convw_ref, gprm_ref, norma_ref,
                     normb_ref, out_ref,
                     proj_ref, xbuf_ref, qkv_ref, gate_ref, gcum_ref, gt_ref, gcumt_ref, gcb_ref, bcb_ref,
                     u_ref, wq_ref, attn_ref, kdect_ref, eg_ref,
                     qbb_ref, pv0_ref, rs0_ref, mib_ref, kn0_ref, bl_ref, wm_ref,
                     sa_ref, cn_ref, mb_ref):
    n_rows, tile = h_ref.shape[0], h_ref.shape[1]
    ts = n_rows * tile
    n_chunks = ts // CHUNK
    chunks_per_row = tile // CHUNK
    t = pl.program_id(1)

    @pl.when(t == 0)
    def _():
        xbuf_ref[:, 0:8, :] = jnp.zeros((n_rows, 8, CONV_CH), F32)
        sa_ref[...] = jnp.zeros_like(sa_ref)
        cn_ref[...] = jnp.zeros_like(cn_ref)
        mb_ref[...] = jnp.zeros_like(mb_ref)

    hn = _rms(h_ref[...].reshape(ts, D_MODEL), g0_ref[...]).astype(BF16)
    gates_pre = jnp.dot(hn, wgate_ref[...], preferred_element_type=F32) + gprm_ref[0:1, :]

    lane = lax.broadcasted_iota(jnp.int32, (1, GATE_COLS), 1)
    neg_a = -jnp.exp(gprm_ref[1:2, :])
    gate = jnp.where(lane < 4, _sigmoid(gates_pre),
                     jnp.where(lane < 8, neg_a * _softplus(gates_pre),
                               jnp.where(lane < 12, gates_pre, -_softplus(-gates_pre))))
    gate_ref[...] = gate
    row = lax.broadcasted_iota(jnp.int32, (CHUNK, CHUNK), 0)
    col = lax.broadcasted_iota(jnp.int32, (CHUNK, CHUNK), 1)
    causal = row >= col
    tri = causal.astype(F32)
    row2 = lax.broadcasted_iota(jnp.int32, (CHUNK, 2 * CHUNK), 0)
    col2 = lax.broadcasted_iota(jnp.int32, (CHUNK, 2 * CHUNK), 1) % CHUNK
    causal2 = row2 >= col2
    strict2 = row2 > col2
    eye2 = (row2 == col2).astype(F32)
    twice = lambda x: jnp.concatenate([x, x], 1)
    for c in range(n_chunks):
        gch = gate_ref[c * CHUNK:(c + 1) * CHUNK, :]
        cum = _dot_hi(tri, gch)
        gcum_ref[c * CHUNK:(c + 1) * CHUNK, :] = cum
        gt_ref[c] = twice(gch.T[0:16, :])
        gcumt_ref[c] = twice(cum.T[0:16, :])
    for hd in range(HEADS_AB):
        gcb_ref[hd] = jnp.broadcast_to(gcum_ref[:, 4 + hd:5 + hd], (ts, HEAD_DIM_AB))
        bcb_ref[hd] = jnp.broadcast_to(gcum_ref[:, 12 + hd:13 + hd], (ts, HEAD_DIM_AB))

    proj_ref[...] = jnp.dot(hn, wbig_ref[...], preferred_element_type=F32)
    proj_ref[:, 3 * QK_AB:4 * QK_AB] = _silu(proj_ref[:, 3 * QK_AB:4 * QK_AB])
    proj_ref[:, 7 * QK_AB:8 * QK_AB] = _sigmoid(proj_ref[:, 7 * QK_AB:8 * QK_AB])

    for r in range(n_rows):
        xbuf_ref[r, 8:8 + tile, :] = proj_ref[r * tile:(r + 1) * tile, 0:CONV_CH]
        xfull = xbuf_ref[r]
        conv = convw_ref[0:1, :] * xfull
        for k in range(1, CONV_K):
            conv = convw_ref[k:k + 1, :] * xfull + pltpu.roll(conv, 1, 0)
        xbuf_ref[r, 0:8, :] = xbuf_ref[r, tile:tile + 8, :]
        qkv_ref[r * tile:(r + 1) * tile, :] = _silu(conv[8:, :])
    for hd in range(HEADS_AB):
        sl = slice(hd * HEAD_DIM_AB, (hd + 1) * HEAD_DIM_AB)
        q = qkv_ref[:, sl]
        qkv_ref[:, sl] = q * lax.rsqrt(jnp.sum(q * q, -1, keepdims=True) + EPS) * (HEAD_DIM_AB ** -0.5)
        sl = slice(QK_AB + hd * HEAD_DIM_AB, QK_AB + (hd + 1) * HEAD_DIM_AB)
        k = qkv_ref[:, sl]
        qkv_ref[:, sl] = k * lax.rsqrt(jnp.sum(k * k, -1, keepdims=True) + EPS)

    norm_a = norma_ref[...]

    def prep_body(i, carry):
        lmats, rhs = [], []
        for j in range(PREP_CHUNKS):
            c = i * PREP_CHUNKS + j
            r0 = pl.multiple_of(c * CHUNK, CHUNK)
            rows = pl.ds(r0, CHUNK)
            gct = gcumt_ref[c]
            for hd in range(HEADS_AB):
                lo, hi = hd * HEAD_DIM_AB, (hd + 1) * HEAD_DIM_AB
                q = qkv_ref[rows, lo:hi]
                k = qkv_ref[rows, QK_AB + lo:QK_AB + hi]
                v = qkv_ref[rows, 2 * QK_AB + lo:2 * QK_AB + hi]
                gtc = gt_ref[c]
                beta_row = gtc[hd:hd + 1, :]
                gc_row = gct[4 + hd:5 + hd, :]
                gc_all = gcb_ref[hd, rows, :]
                g_last = gc_all[CHUNK - 1:CHUNK, :]
                diff = gc_all - gc_row
                decay = jnp.where(causal2, jnp.exp(jnp.where(causal2, diff, 0.0)), 0.0)
                egc = jnp.exp(gc_all)
                k_bf = k.astype(BF16)
                kk = _dot_nt(k_bf, jnp.concatenate([k_bf, k_bf], 0))
                lmats.append(jnp.where(strict2, kk * decay, 0.0) * beta_row)
                rhs.append((c, hd, v.astype(BF16), (k * egc).astype(BF16)))
                wq_ref[c, hd, CHUNK:2 * CHUNK, :] = (q * egc).astype(BF16)
                attn_ref[c, hd] = (_dot_nt(q, k_bf) * (decay * beta_row)[:, 0:CHUNK]).astype(BF16)
                kdect_ref[c, hd] = ((k * jnp.exp(g_last - gc_all)).T * beta_row[:, 0:CHUNK]).astype(BF16)
                eg_ref[c, hd] = jnp.exp(g_last)
                qb = proj_ref[rows, 4 * QK_AB + lo:4 * QK_AB + hi] * (HEAD_DIM_AB ** -0.5)
                kbb = proj_ref[rows, 5 * QK_AB + lo:5 * QK_AB + hi]
                vbb = proj_ref[rows, 6 * QK_AB + lo:6 * QK_AB + hi].astype(BF16)
                bc_all = bcb_ref[hd, rows, :]
                bc_row = gct[12 + hd:13 + hd, 0:CHUNK]
                i_row = gtc[8 + hd:9 + hd, 0:CHUNK]
                b_last = bc_all[CHUNK - 1:CHUNK, :]
                dmat = jnp.where(causal, bc_all[:, 0:CHUNK] - bc_row + i_row, -jnp.inf)
                m_intra = jnp.max(dmat, -1, keepdims=True)
                p0 = _dot_nt(qb, kbb) * jnp.exp(dmat - m_intra)
                qbb_ref[c, hd] = qb.astype(BF16)
                pv0_ref[c, hd] = _dot(p0, vbb)
                rs0_ref[c, hd] = jnp.broadcast_to(jnp.sum(p0, -1, keepdims=True), (CHUNK, HEAD_DIM_AB))
                mib_ref[c, hd] = jnp.broadcast_to(m_intra, (CHUNK, HEAD_DIM_AB))
                ws_row = b_last[:, 0:CHUNK] - bc_row + i_row
                ws_max = jnp.max(ws_row, -1, keepdims=True)
                kw_t = kbb.T * jnp.exp(ws_row - ws_max)
                kn0_ref[c, hd, :, 0:HEAD_DIM_AB] = _dot(kw_t, vbb)
                kn0_ref[c, hd, :, HEAD_DIM_AB:] = jnp.broadcast_to(
                    jnp.sum(kw_t, -1, keepdims=True), (HEAD_DIM_AB, HEAD_DIM_AB))
                bl_ref[c, hd] = jnp.concatenate([b_last, b_last], 1)
                wm_ref[c, hd] = jnp.broadcast_to(ws_max, (1, 2 * HEAD_DIM_AB))
        lmat = jnp.stack(lmats)
        l_split = _split(lmat)
        tinv = eye2 - lmat
        pw = _dot3(l_split, l_split)
        for _ in range(4):
            pw_split = _split(pw)
            tinv = tinv + _dot3(_split(tinv), pw_split)
            pw = _dot3(pw_split, pw_split)
        tinv = (tinv + _dot3(_split(tinv), _split(pw)))[:, :, 0:CHUNK].astype(BF16)
        for n, (c, hd, v_bf, ke_bf) in enumerate(rhs):
            u_ref[c, hd] = jnp.dot(tinv[n], v_bf, preferred_element_type=F32)
            wq_ref[c, hd, 0:CHUNK, :] = jnp.dot(tinv[n], ke_bf, preferred_element_type=F32).astype(BF16)
        return carry

    lax.fori_loop(0, n_chunks // PREP_CHUNKS, prep_body, 0)

    def chunk_body(j, carry):
        local = pl.ds(pl.multiple_of(j * CHUNK, CHUNK), CHUNK)
        rows = [pl.ds(pl.multiple_of(r * tile + j * CHUNK, CHUNK), CHUNK) for r in range(n_rows)]
        both = lambda ref: jnp.concatenate([ref[r * chunks_per_row + j] for r in range(n_rows)], 0)
        bdot = lambda x, y: lax.dot_general(x, y, (((2,), (1,)), ((0,), (0,))),
                                            preferred_element_type=F32)
        state = sa_ref[...]
        ws_qs = bdot(both(wq_ref), state.astype(BF16))
        cn = cn_ref[...]
        q_cn = bdot(both(qbb_ref), cn.astype(BF16))

        v_new = (both(u_ref) - ws_qs[:, 0:CHUNK]).astype(BF16)
        o = ws_qs[:, CHUNK:] + bdot(both(attn_ref), v_new)
        sa_ref[...] = state * both(eg_ref) + bdot(both(kdect_ref), v_new)

        m_prev = mb_ref[:, 0:1, :]
        a = jnp.concatenate([bcb_ref[:, rw, :] for rw in rows], 0) + m_prev[:, :, 0:HEAD_DIM_AB]
        m_intra = both(mib_ref)
        mt = jnp.maximum(a, m_intra)
        inter = jnp.exp(a - mt)
        intra = jnp.exp(m_intra - mt)
        num = inter * q_cn[:, :, 0:HEAD_DIM_AB] + intra * both(pv0_ref)
        den = inter * q_cn[:, :, HEAD_DIM_AB:] + intra * both(rs0_ref)
        hout = num / jnp.maximum(jnp.abs(den), jnp.exp(-mt))
        b_last, ws_max = both(bl_ref), both(wm_ref)
        m_new = jnp.maximum(b_last + m_prev, ws_max)
        cn_ref[...] = jnp.exp(b_last + m_prev - m_new) * cn + jnp.exp(ws_max - m_new) * both(kn0_ref)
        mb_ref[...] = jnp.broadcast_to(m_new, mb_ref.shape)

        for r in range(n_rows):
            for hd in range(HEADS_AB):
                n = r * HEADS_AB + hd
                lo, hi = hd * HEAD_DIM_AB, (hd + 1) * HEAD_DIM_AB
                z_gate = proj_ref[rows[r], 3 * QK_AB + lo:3 * QK_AB + hi]
                out_ref[r, local, lo:hi] = (_rms(o[n], norm_a) * z_gate).astype(out_ref.dtype)
                o_gate = proj_ref[rows[r], 7 * QK_AB + lo:7 * QK_AB + hi]
                out_ref[r, local, QK_AB + lo:QK_AB + hi] = (
                    _rms(hout[n], normb_ref[hd:hd + 1, :]) * o_gate).astype(out_ref.dtype)
        return carry

    lax.fori_loop(0, chunks_per_row, chunk_body, 0, unroll=SEQ_UNROLL)


def _mixer_ab(h, g0, w_in, conv_w, a_log, dt_bias, norm_a, i_bias, f_bias, norm_b, batch, seq):
    tile = min(AB_TILE, seq)
    n_rows = AB_ROWS if batch % AB_ROWS == 0 else 1
    tiles = seq // tile
    ts = n_rows * tile
    a_cols = 2 * QK_AB + QK_AB + 2 * HEADS_AB + QK_AB
    wa, wb = w_in[:, :a_cols], w_in[:, a_cols:]
    g_off = 3 * QK_AB
    z_off = g_off + 2 * HEADS_AB
    w_big = jnp.concatenate([wa[:, :g_off], wa[:, z_off:], wb[:, :g_off], wb[:, z_off:]], 1).astype(BF16)
    w_gate = jnp.concatenate([wa[:, g_off:z_off], wb[:, g_off:z_off],
                              jnp.zeros((D_MODEL, GATE_COLS - 4 * HEADS_AB), F32)], 1).astype(BF16)
    zeros4 = jnp.zeros((HEADS_AB,), F32)
    pad = jnp.zeros((GATE_COLS - 4 * HEADS_AB,), F32)
    bias_row = jnp.concatenate([zeros4, dt_bias, i_bias, f_bias, pad])
    alog_row = jnp.concatenate([zeros4, a_log, zeros4, zeros4, pad])
    gprm = jnp.concatenate([bias_row[None], alog_row[None], jnp.zeros((6, GATE_COLS), F32)], 0)

    const = lambda shape: pl.BlockSpec(shape, lambda b, t: (0,) * len(shape),
                                       pipeline_mode=pl.Buffered(1))
    n_chunks = ts // CHUNK
    per_head = (n_chunks, HEADS_AB)
    all_heads = n_rows * HEADS_AB
    return pl.pallas_call(
        _mixer_ab_kernel,
        grid=(batch // n_rows, tiles),
        in_specs=[
            pl.BlockSpec((n_rows, tile, D_MODEL), lambda b, t: (b, t, 0)),
            const((1, D_MODEL)),
            const((D_MODEL, BIG_COLS)),
            const((D_MODEL, GATE_COLS)),
            const((CONV_K, CONV_CH)),
            const((8, GATE_COLS)),
            const((1, HEAD_DIM_AB)),
            const((HEADS_AB, HEAD_DIM_AB)),
        ],
        out_specs=pl.BlockSpec((n_rows, tile, 2 * QK_AB), lambda b, t: (b, t, 0)),
        out_shape=jax.ShapeDtypeStruct((batch, seq, 2 * QK_AB), BF16),
        scratch_shapes=[
            pltpu.VMEM((ts, BIG_COLS), F32),
            pltpu.VMEM((n_rows, tile + 8, CONV_CH), F32),
            pltpu.VMEM((ts, CONV_CH), F32),
            pltpu.VMEM((ts, GATE_COLS), F32),
            pltpu.VMEM((ts, GATE_COLS), F32),
            pltpu.VMEM((n_chunks, 16, 2 * CHUNK), F32),
            pltpu.VMEM((n_chunks, 16, 2 * CHUNK), F32),
            pltpu.VMEM((HEADS_AB, ts, HEAD_DIM_AB), F32),
            pltpu.VMEM((HEADS_AB, ts, HEAD_DIM_AB), F32),
            pltpu.VMEM(per_head + (CHUNK, HEAD_DIM_AB), F32),
            pltpu.VMEM(per_head + (2 * CHUNK, HEAD_DIM_AB), BF16),
            pltpu.VMEM(per_head + (CHUNK, CHUNK), BF16),
            pltpu.VMEM(per_head + (HEAD_DIM_AB, CHUNK), BF16),
            pltpu.VMEM(per_head + (1, HEAD_DIM_AB), F32),
            pltpu.VMEM(per_head + (CHUNK, HEAD_DIM_AB), BF16),
            pltpu.VMEM(per_head + (CHUNK, HEAD_DIM_AB), F32),
            pltpu.VMEM(per_head + (CHUNK, HEAD_DIM_AB), F32),
            pltpu.VMEM(per_head + (CHUNK, HEAD_DIM_AB), F32),
            pltpu.VMEM(per_head + (HEAD_DIM_AB, 2 * HEAD_DIM_AB), F32),
            pltpu.VMEM(per_head + (1, 2 * HEAD_DIM_AB), F32),
            pltpu.VMEM(per_head + (1, 2 * HEAD_DIM_AB), F32),
            pltpu.VMEM((all_heads, HEAD_DIM_AB, HEAD_DIM_AB), F32),
            pltpu.VMEM((all_heads, HEAD_DIM_AB, 2 * HEAD_DIM_AB), F32),
            pltpu.VMEM((all_heads, 8, 2 * HEAD_DIM_AB), F32),
        ],
        compiler_params=pltpu.CompilerParams(
            dimension_semantics=("arbitrary", "arbitrary"),
            vmem_limit_bytes=VMEM_LIMIT_BYTES),
        name="mixer_ab",
    )(h.reshape(batch, seq, D_MODEL), g0[None], w_big, w_gate, conv_w, gprm, norm_a[None],
      norm_b).reshape(batch * seq, 2 * QK_AB)


def _mixer_c_kernel(h_ref, pos_ref, invc_ref, sink_ref, g0_ref, wqt_ref, bq_ref, wkt_ref, bk_ref,
                    wvt_ref, bv_ref, out_ref, q_ref, klo_ref, khi_ref, vtlo_ref, vthi_ref):
    ts = h_ref.shape[0]
    n_blocks = ts // WINDOW
    half = C_HD // 2
    t = pl.program_id(1)

    @pl.when(t == 0)
    def _():
        vtlo_ref[...] = jnp.zeros_like(vtlo_ref)
        vthi_ref[...] = jnp.zeros_like(vthi_ref)
        klo_ref[0:WINDOW, :] = jnp.zeros((WINDOW, klo_ref.shape[1]), BF16)
        khi_ref[0:WINDOW, :] = jnp.zeros((WINDOW, khi_ref.shape[1]), BF16)

    hn = _rms(h_ref[...], g0_ref[...]).astype(BF16)

    ang_t = pos_ref[...].astype(F32) * invc_ref[...]
    cos_t = jnp.cos(ang_t)
    sin_t = jnp.sin(ang_t)

    def rope_t(x):
        x1, x2 = x[0:half], x[half:C_HD]
        return jnp.concatenate([x1 * cos_t - x2 * sin_t, x2 * cos_t + x1 * sin_t], 0)

    qt = _dot_nt(wqt_ref[...], hn) + bq_ref[...]
    for hd in range(C_HEADS):
        rows = slice(hd * C_HD, (hd + 1) * C_HD)
        q_ref[rows, :] = (rope_t(qt[rows]) * (C_HD ** -0.5)).astype(BF16)

    kt = _dot_nt(wkt_ref[...], hn) + bk_ref[...]
    kk = jnp.concatenate([rope_t(kt[j * C_HD:(j + 1) * C_HD]) for j in range(2 * C_KV_HEADS)], 0).T
    low = (lax.broadcasted_iota(jnp.int32, (1, 2 * C_KV), 1) % 128) < C_HD
    klo_ref[WINDOW:, :] = jnp.where(low, kk, 0.0).astype(BF16)
    khi_ref[WINDOW:, :] = jnp.where(low, 0.0, kk).astype(BF16)

    vt = (_dot_nt(wvt_ref[...], hn) + bv_ref[...]).astype(BF16)
    for g in range(C_KV_HEADS):
        vtlo_ref[2 * g * C_HD:(2 * g + 1) * C_HD, WINDOW:] = vt[g * C_HD:(g + 1) * C_HD]
        vthi_ref[(2 * g + 1) * C_HD:(2 * g + 2) * C_HD, WINDOW:] = vt[g * C_HD:(g + 1) * C_HD]

    bdot = lambda x, y: lax.dot_general(x, y, (((2,), (1,)), ((0,), (0,))), preferred_element_type=F32)
    groups = range(C_KV_HEADS)
    key_idx = lax.broadcasted_iota(jnp.int32, (WINDOW, 2 * WINDOW), 0)
    qry_idx = lax.broadcasted_iota(jnp.int32, (WINDOW, 2 * WINDOW), 1) % WINDOW
    from_prev = key_idx > qry_idx
    def scores(blk):
        cols = slice(blk * WINDOW, (blk + 1) * WINDOW)
        win = slice(blk * WINDOW, (blk + 2) * WINDOW)
        kbd = jnp.stack([jnp.concatenate([klo_ref[win, g * 128:(g + 1) * 128],
                                          khi_ref[win, g * 128:(g + 1) * 128]], 0) for g in groups])
        qbd = jnp.stack([jnp.concatenate([q_ref[g * 256:g * 256 + 128, cols],
                                          q_ref[g * 256 + 128:(g + 1) * 256, cols]], 1) for g in groups])
        return bdot(kbd, qbd)

    pending = [scores(blk) for blk in range(min(SCORES_AHEAD, n_blocks))]
    for blk in range(n_blocks):
        cols = slice(blk * WINDOW, (blk + 1) * WINDOW)
        win = slice(blk * WINDOW, (blk + 2) * WINDOW)
        st = pending.pop(0)
        if blk + SCORES_AHEAD < n_blocks:
            pending.append(scores(blk + SCORES_AHEAD))
        probs = []
        for odd in range(2):
            s_prev = st[:, odd * 2 * WINDOW:odd * 2 * WINDOW + WINDOW]
            s_cur = st[:, odd * 2 * WINDOW + WINDOW:(odd + 1) * 2 * WINDOW]
            sc = jnp.where(from_prev, s_prev, s_cur)
            if blk == 0:
                sc = jnp.where(from_prev & (key_idx < jnp.where(t > 0, -1, WINDOW)), -jnp.inf, sc)
            sink = sink_ref[odd]
            m = jnp.maximum(jnp.max(sc, 1, keepdims=True), sink)
            pr = jnp.exp(sc - m)
            den = jnp.sum(pr, 1, keepdims=True) + jnp.exp(sink - m)
            pr = pr / den
            probs += [jnp.where(from_prev, pr, 0.0).astype(BF16), jnp.where(from_prev, 0.0, pr).astype(BF16)]
        vbd = jnp.stack([jnp.concatenate([vtlo_ref[g * 128:(g + 1) * 128, win],
                                          vthi_ref[g * 128:(g + 1) * 128, win]], 1) for g in groups])
        ot = bdot(vbd, jnp.concatenate(probs, 1))
        for g in groups:
            o = ot[g].T
            out_ref[cols, g * 256:g * 256 + 128] = o[0:WINDOW].astype(out_ref.dtype)
            out_ref[cols, g * 256 + 128:(g + 1) * 256] = o[WINDOW:].astype(out_ref.dtype)

    klo_ref[0:WINDOW, :] = klo_ref[ts:ts + WINDOW, :]
    khi_ref[0:WINDOW, :] = khi_ref[ts:ts + WINDOW, :]
    vtlo_ref[:, 0:WINDOW] = vtlo_ref[:, ts:ts + WINDOW]
    vthi_ref[:, 0:WINDOW] = vthi_ref[:, ts:ts + WINDOW]


def _mixer_c(h, g0, positions, w_qkv, b_qkv, sinks, batch, seq):
    tokens = batch * seq
    ts = min(C_TILE, seq)
    tiles = seq // ts
    half = C_HD // 2
    inv_col = (ROPE_THETA ** (-jnp.arange(half, dtype=F32) / half))[:, None]
    pos = positions.reshape(batch * tiles, 1, ts)
    wt, b_col = w_qkv.T.astype(BF16), b_qkv[:, None]
    w_q, b_q = wt[:C_Q], b_col[:C_Q]
    dup = lambda a: jnp.repeat(a.reshape(C_KV_HEADS, 1, C_HD, -1), 2, 1).reshape(2 * C_KV, -1)
    w_k, b_k = dup(wt[C_Q:C_Q + C_KV]), dup(b_col[C_Q:C_Q + C_KV])
    w_v, b_v = wt[C_Q + C_KV:], b_col[C_Q + C_KV:]
    sink_rows = jnp.repeat(sinks.reshape(C_KV_HEADS, 2, 2).transpose(2, 0, 1), WINDOW, -1)[:, :, None, :]
    const = lambda shape: pl.BlockSpec(shape, lambda b, t: (0,) * len(shape),
                                       pipeline_mode=pl.Buffered(1))
    return pl.pallas_call(
        _mixer_c_kernel,
        grid=(batch, tiles),
        in_specs=[
            pl.BlockSpec((ts, D_MODEL), lambda b, t: (b * tiles + t, 0)),
            pl.BlockSpec((None, 1, ts), lambda b, t: (b * tiles + t, 0, 0)),
            const((half, 1)),
            const((2, C_KV_HEADS, 1, 2 * WINDOW)),
            const((1, D_MODEL)),
            const((C_Q, D_MODEL)),
            const((C_Q, 1)),
            const((2 * C_KV, D_MODEL)),
            const((2 * C_KV, 1)),
            const((C_KV, D_MODEL)),
            const((C_KV, 1)),
        ],
        out_specs=pl.BlockSpec((ts, C_Q), lambda b, t: (b * tiles + t, 0)),
        out_shape=jax.ShapeDtypeStruct((tokens, C_Q), BF16),
        scratch_shapes=[
            pltpu.VMEM((C_Q, ts), BF16),
            pltpu.VMEM((WINDOW + ts, 2 * C_KV), BF16),
            pltpu.VMEM((WINDOW + ts, 2 * C_KV), BF16),
            pltpu.VMEM((2 * C_KV, WINDOW + ts), BF16),
            pltpu.VMEM((2 * C_KV, WINDOW + ts), BF16),
        ],
        compiler_params=pltpu.CompilerParams(
            dimension_semantics=("arbitrary", "arbitrary"),
            vmem_limit_bytes=VMEM_LIMIT_BYTES),
        name="mixer_c",
    )(h, pos, inv_col, sink_rows, g0[None], w_q, b_q, w_k, b_k, w_v, b_v)


def _post_kernel(mix_ref, h_ref, p_ref, wout_ref, bout_ref, gains_ref, wup_ref, wdown_ref,
                 wgate_ref, wple_ref, out_ref):
    mix = jnp.dot(mix_ref[...], wout_ref[...], preferred_element_type=F32) + bout_ref[...]
    h = h_ref[...] + _rms(mix, gains_ref[1:2, :])
    hn = _rms(h, gains_ref[2:3, :]).astype(BF16)
    slab = D_FF // FF_SPLIT
    ff = None
    for j in range(FF_SPLIT):
        up = jnp.dot(hn, wup_ref[:, j * slab:(j + 1) * slab], preferred_element_type=F32)
        act = jnp.square(jnp.maximum(up, 0.0)).astype(BF16)
        part = jnp.dot(act, wdown_ref[j * slab:(j + 1) * slab, :], preferred_element_type=F32)
        ff = part if ff is None else ff + part
    h = h + _rms(ff, gains_ref[3:4, :])
    gate = _sigmoid(jnp.dot(h.astype(BF16), wgate_ref[...], preferred_element_type=F32))
    ple = jnp.dot(p_ref[...].astype(BF16), wple_ref[...], preferred_element_type=F32)
    out_ref[...] = h + gate * ple


def _post(mixed, h, p_all, layer, w_out, b_out, gains, w_up, w_down, w_gate, w_ple):
    tokens = h.shape[0]
    tm = min(POST_TILE, tokens)
    const = lambda shape: pl.BlockSpec(shape, lambda i: (0,) * len(shape),
                                       pipeline_mode=pl.Buffered(1))
    return pl.pallas_call(
        _post_kernel,
        grid=(tokens // tm,),
        in_specs=[
            pl.BlockSpec((tm, D_MODEL), lambda i: (i, 0)),
            pl.BlockSpec((tm, D_MODEL), lambda i: (i, 0)),
            pl.BlockSpec((None, tm, PLE_DIM), lambda i: (layer, i, 0)),
            const((D_MODEL, D_MODEL)),
            const((1, D_MODEL)),
            const((4, D_MODEL)),
            const((D_MODEL, D_FF)),
            const((D_FF, D_MODEL)),
            const((D_MODEL, D_MODEL)),
            const((PLE_DIM, D_MODEL)),
        ],
        out_specs=pl.BlockSpec((tm, D_MODEL), lambda i: (i, 0)),
        out_shape=jax.ShapeDtypeStruct((tokens, D_MODEL), F32),
        compiler_params=pltpu.CompilerParams(
            dimension_semantics=("arbitrary",),
            vmem_limit_bytes=VMEM_LIMIT_BYTES),
        name="post",
    )(mixed, h, p_all, w_out.astype(BF16), b_out[None], gains, w_up.astype(BF16),
      w_down.astype(BF16), w_gate.astype(BF16), w_ple.astype(BF16))


def kernel(x, p, positions, norm_gains, w_in_ab, conv_a, a_log, dt_bias, norm_a, i_bias_b, f_bias_b, norm_b, w_out_ab, w_qkv_c, b_qkv_c, sinks_c, w_o_c, b_o_c, w_up, w_down, w_ple, w_ple_gate):
    batch, seq, _ = x.shape
    depth = norm_gains.shape[0]
    tokens = batch * seq
    h = x.reshape(tokens, D_MODEL)
    p_all = p.reshape(depth, tokens, PLE_DIM)
    for layer in range(depth):
        gains = norm_gains[layer]
        if layer % 2 == 0:
            e = layer // 2
            mixed = _mixer_ab(h, gains[0], w_in_ab[e], conv_a[e], a_log[e], dt_bias[e], norm_a[e],
                              i_bias_b[e], f_bias_b[e], norm_b[e], batch, seq)
            w_out, b_out = w_out_ab[e], jnp.zeros((D_MODEL,), F32)
        else:
            o = layer // 2
            mixed = _mixer_c(h, gains[0], positions, w_qkv_c[o], b_qkv_c[o], sinks_c[o], batch, seq)
            w_out, b_out = w_o_c[o], b_o_c[o]
        h = _post(mixed, h, p_all, layer, w_out, b_out, gains, w_up[layer], w_down[layer],
                  w_ple_gate[layer], w_ple[layer])
    return h.reshape(batch, seq, D_MODEL)
```

```python
import jax
import jax.numpy as jnp
from jax import lax
from jax.experimental import pallas as pl
from jax.experimental.pallas import tpu as pltpu

F32 = jnp.float32
BF16 = jnp.bfloat16
HIGHEST = lax.Precision.HIGHEST

EPS = 1e-6
D_MODEL = 1024
HEADS_AB = 4
HEAD_DIM_AB = 128
CONV_K = 4
CHUNK = 64
C_HEADS = 16
C_KV_HEADS = 4
C_HD = 64
WINDOW = 128
ROPE_THETA = 10000.0
D_FF = 4 * D_MODEL
PLE_DIM = 256
QK_AB = HEADS_AB * HEAD_DIM_AB
CONV_CH = 3 * QK_AB
BIG_COLS = 8 * QK_AB
GATE_COLS = 128
C_Q = C_HEADS * C_HD
C_KV = C_KV_HEADS * C_HD

AB_ROWS = 2
AB_TILE = 256
C_TILE = 2048
POST_TILE = 1024
PREP_CHUNKS = 8
SCORES_AHEAD = 8
SEQ_UNROLL = 4
FF_SPLIT = 8
VMEM_LIMIT_BYTES = 56 * 1024 * 1024


def _rms(x, g):
    return x * lax.rsqrt(jnp.mean(x * x, -1, keepdims=True) + EPS) * g


def _dot(a, b):
    return jnp.dot(a.astype(BF16), b.astype(BF16), preferred_element_type=F32)


def _dot_nt(a, b):
    return lax.dot_general(a.astype(BF16), b.astype(BF16), (((1,), (1,)), ((), ())),
                           preferred_element_type=F32)


def _dot_hi(a, b):
    return jnp.dot(a, b, preferred_element_type=F32, precision=HIGHEST)


def _split(a):
    hi = a.astype(BF16)
    hi_f = hi.astype(F32)
    lo_f = a - hi_f
    left = lax.broadcasted_iota(jnp.int32, a.shape, 2) < a.shape[2] // 2
    return hi, lo_f.astype(BF16), jnp.where(left, hi_f, lo_f).astype(BF16)


def _dot3(a, b):
    (ah, _, ahl), (bh, bl, _) = a, b
    lhs = jnp.concatenate([ahl, ah], 2)
    rhs = jnp.concatenate([bl, bh, bh, jnp.zeros_like(bh)], 1)
    return lax.dot_general(lhs, rhs, (((2,), (1,)), ((0,), (0,))), preferred_element_type=F32)


def _softplus(x):
    return jnp.maximum(x, 0.0) + jnp.log1p(jnp.exp(-jnp.abs(x)))


def _sigmoid(x):
    return 1.0 / (1.0 + jnp.exp(-x))


def _silu(x):
    return x * _sigmoid(x)


def _mixer_ab_kernel(h_ref, g0_ref, wbig_ref, wgate_ref, convw_ref, gprm_ref, norma_ref,
                     normb_ref, out_ref,
                     proj_ref, xbuf_ref, qkv_ref, gate_ref, gcum_ref, gt_ref, gcumt_ref, gcb_ref, bcb_ref,
                     u_ref, wq_ref, attn_ref, kdect_ref, eg_ref,
                     qbb_ref, pv0_ref, rs0_ref, mib_ref, kn0_ref, bl_ref, wm_ref,
                     sa_ref, cn_ref, mb_ref):
    n_rows, tile = h_ref.shape[0], h_ref.shape[1]
    ts = n_rows * tile
    n_chunks = ts // CHUNK
    chunks_per_row = tile // CHUNK
    t = pl.program_id(1)

    @pl.when(t == 0)
    def _():
        xbuf_ref[:, 0:8, :] = jnp.zeros((n_rows, 8, CONV_CH), F32)
        sa_ref[...] = jnp.zeros_like(sa_ref)
        cn_ref[...] = jnp.zeros_like(cn_ref)
        mb_ref[...] = jnp.zeros_like(mb_ref)

    hn = _rms(h_ref[...].reshape(ts, D_MODEL), g0_ref[...]).astype(BF16)
    gates_pre = jnp.dot(hn, wgate_ref[...], preferred_element_type=F32) + gprm_ref[0:1, :]

    lane = lax.broadcasted_iota(jnp.int32, (1, GATE_COLS), 1)
    neg_a = -jnp.exp(gprm_ref[1:2, :])
    gate = jnp.where(lane < 4, _sigmoid(gates_pre),
                     jnp.where(lane < 8, neg_a * _softplus(gates_pre),
                               jnp.where(lane < 12, gates_pre, -_softplus(-gates_pre))))
    gate_ref[...] = gate
    row = lax.broadcasted_iota(jnp.int32, (CHUNK, CHUNK), 0)
    col = lax.broadcasted_iota(jnp.int32, (CHUNK, CHUNK), 1)
    causal = row >= col
    tri = causal.astype(F32)
    row2 = lax.broadcasted_iota(jnp.int32, (CHUNK, 2 * CHUNK), 0)
    col2 = lax.broadcasted_iota(jnp.int32, (CHUNK, 2 * CHUNK), 1) % CHUNK
    causal2 = row2 >= col2
    strict2 = row2 > col2
    eye2 = (row2 == col2).astype(F32)
    twice = lambda x: jnp.concatenate([x, x], 1)
    for c in range(n_chunks):
        gch = gate_ref[c * CHUNK:(c + 1) * CHUNK, :]
        cum = _dot_hi(tri, gch)
        gcum_ref[c * CHUNK:(c + 1) * CHUNK, :] = cum
        gt_ref[c] = twice(gch.T[0:16, :])
        gcumt_ref[c] = twice(cum.T[0:16, :])
    for hd in range(HEADS_AB):
        gcb_ref[hd] = jnp.broadcast_to(gcum_ref[:, 4 + hd:5 + hd], (ts, HEAD_DIM_AB))
        bcb_ref[hd] = jnp.broadcast_to(gcum_ref[:, 12 + hd:13 + hd], (ts, HEAD_DIM_AB))

    proj_ref[...] = jnp.dot(hn, wbig_ref[...], preferred_element_type=F32)
    proj_ref[:, 3 * QK_AB:4 * QK_AB] = _silu(proj_ref[:, 3 * QK_AB:4 * QK_AB])
    proj_ref[:, 7 * QK_AB:8 * QK_AB] = _sigmoid(proj_ref[:, 7 * QK_AB:8 * QK_AB])

    for r in range(n_rows):
        xbuf_ref[r, 8:8 + tile, :] = proj_ref[r * tile:(r + 1) * tile, 0:CONV_CH]
        xfull = xbuf_ref[r]
        conv = convw_ref[0:1, :] * xfull
        for k in range(1, CONV_K):
            conv = convw_ref[k:k + 1, :] * xfull + pltpu.roll(conv, 1, 0)
        xbuf_ref[r, 0:8, :] = xbuf_ref[r, tile:tile + 8, :]
        qkv_ref[r * tile:(r + 1) * tile, :] = _silu(conv[8:, :])
    for hd in range(HEADS_AB):
        sl = slice(hd * HEAD_DIM_AB, (hd + 1) * HEAD_DIM_AB)
        q = qkv_ref[:, sl]
        qkv_ref[:, sl] = q * lax.rsqrt(jnp.sum(q * q, -1, keepdims=True) + EPS) * (HEAD_DIM_AB ** -0.5)
        sl = slice(QK_AB + hd * HEAD_DIM_AB, QK_AB + (hd + 1) * HEAD_DIM_AB)
        k = qkv_ref[:, sl]
        qkv_ref[:, sl] = k * lax.rsqrt(jnp.sum(k * k, -1, keepdims=True) + EPS)

    norm_a = norma_ref[...]

    def prep_body(i, carry):
        lmats, rhs = [], []
        for j in range(PREP_CHUNKS):
            c = i * PREP_CHUNKS + j
            r0 = pl.multiple_of(c * CHUNK, CHUNK)
            rows = pl.ds(r0, CHUNK)
            gct = gcumt_ref[c]
            for hd in range(HEADS_AB):
                lo, hi = hd * HEAD_DIM_AB, (hd + 1) * HEAD_DIM_AB
                q = qkv_ref[rows, lo:hi]
                k = qkv_ref[rows, QK_AB + lo:QK_AB + hi]
                v = qkv_ref[rows, 2 * QK_AB + lo:2 * QK_AB + hi]
                gtc = gt_ref[c]
                beta_row = gtc[hd:hd + 1, :]
                gc_row = gct[4 + hd:5 + hd, :]
                gc_all = gcb_ref[hd, rows, :]
                g_last = gc_all[CHUNK - 1:CHUNK, :]
                diff = gc_all - gc_row
                decay = jnp.where(causal2, jnp.exp(jnp.where(causal2, diff, 0.0)), 0.0)
                egc = jnp.exp(gc_all)
                k_bf = k.astype(BF16)
                kk = _dot_nt(k_bf, jnp.concatenate([k_bf, k_bf], 0))
                lmats.append(jnp.where(strict2, kk * decay, 0.0) * beta_row)
                rhs.append((c, hd, v.astype(BF16), (k * egc).astype(BF16)))
                wq_ref[c, hd, CHUNK:2 * CHUNK, :] = (q * egc).astype(BF16)
                attn_ref[c, hd] = (_dot_nt(q, k_bf) * (decay * beta_row)[:, 0:CHUNK]).astype(BF16)
                kdect_ref[c, hd] = ((k * jnp.exp(g_last - gc_all)).T * beta_row[:, 0:CHUNK]).astype(BF16)
                eg_ref[c, hd] = jnp.exp(g_last)
                qb = proj_ref[rows, 4 * QK_AB + lo:4 * QK_AB + hi] * (HEAD_DIM_AB ** -0.5)
                kbb = proj_ref[rows, 5 * QK_AB + lo:5 * QK_AB + hi]
                vbb = proj_ref[rows, 6 * QK_AB + lo:6 * QK_AB + hi].astype(BF16)
                bc_all = bcb_ref[hd, rows, :]
                bc_row = gct[12 + hd:13 + hd, 0:CHUNK]
                i_row = gtc[8 + hd:9 + hd, 0:CHUNK]
                b_last = bc_all[CHUNK - 1:CHUNK, :]
                dmat = jnp.where(causal, bc_all[:, 0:CHUNK] - bc_row + i_row, -jnp.inf)
                m_intra = jnp.max(dmat, -1, keepdims=True)
                p0 = _dot_nt(qb, kbb) * jnp.exp(dmat - m_intra)
                qbb_ref[c, hd] = qb.astype(BF16)
                pv0_ref[c, hd] = _dot(p0, vbb)
                rs0_ref[c, hd] = jnp.broadcast_to(jnp.sum(p0, -1, keepdims=True), (CHUNK, HEAD_DIM_AB))
                mib_ref[c, hd] = jnp.broadcast_to(m_intra, (CHUNK, HEAD_DIM_AB))
                ws_row = b_last[:, 0:CHUNK] - bc_row + i_row
                ws_max = jnp.max(ws_row, -1, keepdims=True)
                kw_t = kbb.T * jnp.exp(ws_row - ws_max)
                kn0_ref[c, hd, :, 0:HEAD_DIM_AB] = _dot(kw_t, vbb)
                kn0_ref[c, hd, :, HEAD_DIM_AB:] = jnp.broadcast_to(
                    jnp.sum(kw_t, -1, keepdims=True), (HEAD_DIM_AB, HEAD_DIM_AB))
                bl_ref[c, hd] = jnp.concatenate([b_last, b_last], 1)
                wm_ref[c, hd] = jnp.broadcast_to(ws_max, (1, 2 * HEAD_DIM_AB))
        lmat = jnp.stack(lmats)
        l_split = _split(lmat)
        tinv = eye2 - lmat
        pw = _dot3(l_split, l_split)
        for _ in range(4):
            pw_split = _split(pw)
            tinv = tinv + _dot3(_split(tinv), pw_split)
            pw = _dot3(pw_split, pw_split)
        tinv = (tinv + _dot3(_split(tinv), _split(pw)))[:, :, 0:CHUNK].astype(BF16)
        for n, (c, hd, v_bf, ke_bf) in enumerate(rhs):
            u_ref[c, hd] = jnp.dot(tinv[n], v_bf, preferred_element_type=F32)
            wq_ref[c, hd, 0:CHUNK, :] = jnp.dot(tinv[n], ke_bf, preferred_element_type=F32).astype(BF16)
        return carry

    lax.fori_loop(0, n_chunks // PREP_CHUNKS, prep_body, 0)

    def chunk_body(j, carry):
        local = pl.ds(pl.multiple_of(j * CHUNK, CHUNK), CHUNK)
        rows = [pl.ds(pl.multiple_of(r * tile + j * CHUNK, CHUNK), CHUNK) for r in range(n_rows)]
        both = lambda ref: jnp.concatenate([ref[r * chunks_per_row + j] for r in range(n_rows)], 0)
        bdot = lambda x, y: lax.dot_general(x, y, (((2,), (1,)), ((0,), (0,))),
                                            preferred_element_type=F32)
        state = sa_ref[...]
        ws_qs = bdot(both(wq_ref), state.astype(BF16))
        cn = cn_ref[...]
        q_cn = bdot(both(qbb_ref), cn.astype(BF16))

        v_new = (both(u_ref) - ws_qs[:, 0:CHUNK]).astype(BF16)
        o = ws_qs[:, CHUNK:] + bdot(both(attn_ref), v_new)
        sa_ref[...] = state * both(eg_ref) + bdot(both(kdect_ref), v_new)

        m_prev = mb_ref[:, 0:1, :]
        a = jnp.concatenate([bcb_ref[:, rw, :] for rw in rows], 0) + m_prev[:, :, 0:HEAD_DIM_AB]
        m_intra = both(mib_ref)
        mt = jnp.maximum(a, m_intra)
        inter = jnp.exp(a - mt)
        intra = jnp.exp(m_intra - mt)
        num = inter * q_cn[:, :, 0:HEAD_DIM_AB] + intra * both(pv0_ref)
        den = inter * q_cn[:, :, HEAD_DIM_AB:] + intra * both(rs0_ref)
        hout = num / jnp.maximum(jnp.abs(den), jnp.exp(-mt))
        b_last, ws_max = both(bl_ref), both(wm_ref)
        m_new = jnp.maximum(b_last + m_prev, ws_max)
        cn_ref[...] = jnp.exp(b_last + m_prev - m_new) * cn + jnp.exp(ws_max - m_new) * both(kn0_ref)
        mb_ref[...] = jnp.broadcast_to(m_new, mb_ref.shape)

        for r in range(n_rows):
            for hd in range(HEADS_AB):
                n = r * HEADS_AB + hd
                lo, hi = hd * HEAD_DIM_AB, (hd + 1) * HEAD_DIM_AB
                z_gate = proj_ref[rows[r], 3 * QK_AB + lo:3 * QK_AB + hi]
                out_ref[r, local, lo:hi] = (_rms(o[n], norm_a) * z_gate).astype(out_ref.dtype)
                o_gate = proj_ref[rows[r], 7 * QK_AB + lo:7 * QK_AB + hi]
                out_ref[r, local, QK_AB + lo:QK_AB + hi] = (
                    _rms(hout[n], normb_ref[hd:hd + 1, :]) * o_gate).astype(out_ref.dtype)
        return carry

    lax.fori_loop(0, chunks_per_row, chunk_body, 0, unroll=SEQ_UNROLL)


def _mixer_ab(h, g0, w_in, conv_w, a_log, dt_bias, norm_a, i_bias, f_bias, norm_b, batch, seq):
    tile = min(AB_TILE, seq)
    n_rows = AB_ROWS if batch % AB_ROWS == 0 else 1
    tiles = seq // tile
    ts = n_rows * tile
    a_cols = 2 * QK_AB + QK_AB + 2 * HEADS_AB + QK_AB
    wa, wb = w_in[:, :a_cols], w_in[:, a_cols:]
    g_off = 3 * QK_AB
    z_off = g_off + 2 * HEADS_AB
    w_big = jnp.concatenate([wa[:, :g_off], wa[:, z_off:], wb[:, :g_off], wb[:, z_off:]], 1).astype(BF16)
    w_gate = jnp.concatenate([wa[:, g_off:z_off], wb[:, g_off:z_off],
                              jnp.zeros((D_MODEL, GATE_COLS - 4 * HEADS_AB), F32)], 1).astype(BF16)
    zeros4 = jnp.zeros((HEADS_AB,), F32)
    pad = jnp.zeros((GATE_COLS - 4 * HEADS_AB,), F32)
    bias_row = jnp.concatenate([zeros4, dt_bias, i_bias, f_bias, pad])
    alog_row = jnp.concatenate([zeros4, a_log, zeros4, zeros4, pad])
    gprm = jnp.concatenate([bias_row[None], alog_row[None], jnp.zeros((6, GATE_COLS), F32)], 0)

    const = lambda shape: pl.BlockSpec(shape, lambda b, t: (0,) * len(shape),
                                       pipeline_mode=pl.Buffered(1))
    n_chunks = ts // CHUNK
    per_head = (n_chunks, HEADS_AB)
    all_heads = n_rows * HEADS_AB
    return pl.pallas_call(
        _mixer_ab_kernel,
        grid=(batch // n_rows, tiles),
        in_specs=[
            pl.BlockSpec((n_rows, tile, D_MODEL), lambda b, t: (b, t, 0)),
            const((1, D_MODEL)),
            const((D_MODEL, BIG_COLS)),
            const((D_MODEL, GATE_COLS)),
            const((CONV_K, CONV_CH)),
            const((8, GATE_COLS)),
            const((1, HEAD_DIM_AB)),
            const((HEADS_AB, HEAD_DIM_AB)),
        ],
        out_specs=pl.BlockSpec((n_rows, tile, 2 * QK_AB), lambda b, t: (b, t, 0)),
        out_shape=jax.ShapeDtypeStruct((batch, seq, 2 * QK_AB), BF16),
        scratch_shapes=[
            pltpu.VMEM((ts, BIG_COLS), F32),
            pltpu.VMEM((n_rows, tile + 8, CONV_CH), F32),
            pltpu.VMEM((ts, CONV_CH), F32),
            pltpu.VMEM((ts, GATE_COLS), F32),
            pltpu.VMEM((ts, GATE_COLS), F32),
            pltpu.VMEM((n_chunks, 16, 2 * CHUNK), F32),
            pltpu.VMEM((n_chunks, 16, 2 * CHUNK), F32),
            pltpu.VMEM((HEADS_AB, ts, HEAD_DIM_AB), F32),
            pltpu.VMEM((HEADS_AB, ts, HEAD_DIM_AB), F32),
            pltpu.VMEM(per_head + (CHUNK, HEAD_DIM_AB), F32),
            pltpu.VMEM(per_head + (2 * CHUNK, HEAD_DIM_AB), BF16),
            pltpu.VMEM(per_head + (CHUNK, CHUNK), BF16),
            pltpu.VMEM(per_head + (HEAD_DIM_AB, CHUNK), BF16),
            pltpu.VMEM(per_head + (1, HEAD_DIM_AB), F32),
            pltpu.VMEM(per_head + (CHUNK, HEAD_DIM_AB), BF16),
            pltpu.VMEM(per_head + (CHUNK, HEAD_DIM_AB), F32),
            pltpu.VMEM(per_head + (CHUNK, HEAD_DIM_AB), F32),
            pltpu.VMEM(per_head + (CHUNK, HEAD_DIM_AB), F32),
            pltpu.VMEM(per_head + (HEAD_DIM_AB, 2 * HEAD_DIM_AB), F32),
            pltpu.VMEM(per_head + (1, 2 * HEAD_DIM_AB), F32),
            pltpu.VMEM(per_head + (1, 2 * HEAD_DIM_AB), F32),
            pltpu.VMEM((all_heads, HEAD_DIM_AB, HEAD_DIM_AB), F32),
            pltpu.VMEM((all_heads, HEAD_DIM_AB, 2 * HEAD_DIM_AB), F32),
            pltpu.VMEM((all_heads, 8, 2 * HEAD_DIM_AB), F32),
        ],
        compiler_params=pltpu.CompilerParams(
            dimension_semantics=("arbitrary", "arbitrary"),
            vmem_limit_bytes=VMEM_LIMIT_BYTES),
        name="mixer_ab",
    )(h.reshape(batch, seq, D_MODEL), g0[None], w_big, w_gate, conv_w, gprm, norm_a[None],
      norm_b).reshape(batch * seq, 2 * QK_AB)


def _mixer_c_kernel(h_ref, pos_ref, invc_ref, sink_ref, g0_ref, wqt_ref, bq_ref, wkt_ref, bk_ref,
                    wvt_ref, bv_ref, out_ref, q_ref, klo_ref, khi_ref, vtlo_ref, vthi_ref):
    ts = h_ref.shape[0]
    n_blocks = ts // WINDOW
    half = C_HD // 2
    t = pl.program_id(1)

    @pl.when(t == 0)
    def _():
        vtlo_ref[...] = jnp.zeros_like(vtlo_ref)
        vthi_ref[...] = jnp.zeros_like(vthi_ref)
        klo_ref[0:WINDOW, :] = jnp.zeros((WINDOW, klo_ref.shape[1]), BF16)
        khi_ref[0:WINDOW, :] = jnp.zeros((WINDOW, khi_ref.shape[1]), BF16)

    hn = _rms(h_ref[...], g0_ref[...]).astype(BF16)

    ang_t = pos_ref[...].astype(F32) * invc_ref[...]
    cos_t = jnp.cos(ang_t)
    sin_t = jnp.sin(ang_t)

    def rope_t(x):
        x1, x2 = x[0:half], x[half:C_HD]
        return jnp.concatenate([x1 * cos_t - x2 * sin_t, x2 * cos_t + x1 * sin_t], 0)

    qt = _dot_nt(wqt_ref[...], hn) + bq_ref[...]
    for hd in range(C_HEADS):
        rows = slice(hd * C_HD, (hd + 1) * C_HD)
        q_ref[rows, :] = (rope_t(qt[rows]) * (C_HD ** -0.5)).astype(BF16)

    kt = _dot_nt(wkt_ref[...], hn) + bk_ref[...]
    kk = jnp.concatenate([rope_t(kt[j * C_HD:(j + 1) * C_HD]) for j in range(2 * C_KV_HEADS)], 0).T
    low = (lax.broadcasted_iota(jnp.int32, (1, 2 * C_KV), 1) % 128) < C_HD
    klo_ref[WINDOW:, :] = jnp.where(low, kk, 0.0).astype(BF16)
    khi_ref[WINDOW:, :] = jnp.where(low, 0.0, kk).astype(BF16)

    vt = (_dot_nt(wvt_ref[...], hn) + bv_ref[...]).astype(BF16)
    for g in range(C_KV_HEADS):
        vtlo_ref[2 * g * C_HD:(2 * g + 1) * C_HD, WINDOW:] = vt[g * C_HD:(g + 1) * C_HD]
        vthi_ref[(2 * g + 1) * C_HD:(2 * g + 2) * C_HD, WINDOW:] = vt[g * C_HD:(g + 1) * C_HD]

    bdot = lambda x, y: lax.dot_general(x, y, (((2,), (1,)), ((0,), (0,))), preferred_element_type=F32)
    groups = range(C_KV_HEADS)
    key_idx = lax.broadcasted_iota(jnp.int32, (WINDOW, 2 * WINDOW), 0)
    qry_idx = lax.broadcasted_iota(jnp.int32, (WINDOW, 2 * WINDOW), 1) % WINDOW
    from_prev = key_idx > qry_idx
    def scores(blk):
        cols = slice(blk * WINDOW, (blk + 1) * WINDOW)
        win = slice(blk * WINDOW, (blk + 2) * WINDOW)
        kbd = jnp.stack([jnp.concatenate([klo_ref[win, g * 128:(g + 1) * 128],
                                          khi_ref[win, g * 128:(g + 1) * 128]], 0) for g in groups])
        qbd = jnp.stack([jnp.concatenate([q_ref[g * 256:g * 256 + 128, cols],
                                          q_ref[g * 256 + 128:(g + 1) * 256, cols]], 1) for g in groups])
        return bdot(kbd, qbd)

    pending = [scores(blk) for blk in range(min(SCORES_AHEAD, n_blocks))]
    for blk in range(n_blocks):
        cols = slice(blk * WINDOW, (blk + 1) * WINDOW)
        win = slice(blk * WINDOW, (blk + 2) * WINDOW)
        st = pending.pop(0)
        if blk + SCORES_AHEAD < n_blocks:
            pending.append(scores(blk + SCORES_AHEAD))
        probs = []
        for odd in range(2):
            s_prev = st[:, odd * 2 * WINDOW:odd * 2 * WINDOW + WINDOW]
            s_cur = st[:, odd * 2 * WINDOW + WINDOW:(odd + 1) * 2 * WINDOW]
            sc = jnp.where(from_prev, s_prev, s_cur)
            if blk == 0:
                sc = jnp.where(from_prev & (key_idx < jnp.where(t > 0, -1, WINDOW)), -jnp.inf, sc)
            sink = sink_ref[odd]
            m = jnp.maximum(jnp.max(sc, 1, keepdims=True), sink)
            pr = jnp.exp(sc - m)
            den = jnp.sum(pr, 1, keepdims=True) + jnp.exp(sink - m)
            pr = pr / den
            probs += [jnp.where(from_prev, pr, 0.0).astype(BF16), jnp.where(from_prev, 0.0, pr).astype(BF16)]
        vbd = jnp.stack([jnp.concatenate([vtlo_ref[g * 128:(g + 1) * 128, win],
                                          vthi_ref[g * 128:(g + 1) * 128, win]], 1) for g in groups])
        ot = bdot(vbd, jnp.concatenate(probs, 1))
        for g in groups:
            o = ot[g].T
            out_ref[cols, g * 256:g * 256 + 128] = o[0:WINDOW].astype(out_ref.dtype)
            out_ref[cols, g * 256 + 128:(g + 1) * 256] = o[WINDOW:].astype(out_ref.dtype)

    klo_ref[0:WINDOW, :] = klo_ref[ts:ts + WINDOW, :]
    khi_ref[0:WINDOW, :] = khi_ref[ts:ts + WINDOW, :]
    vtlo_ref[:, 0:WINDOW] = vtlo_ref[:, ts:ts + WINDOW]
    vthi_ref[:, 0:WINDOW] = vthi_ref[:, ts:ts + WINDOW]


def _mixer_c(h, g0, positions, w_qkv, b_qkv, sinks, batch, seq):
    tokens = batch * seq
    ts = min(C_TILE, seq)
    tiles = seq // ts
    half = C_HD // 2
    inv_col = (ROPE_THETA ** (-jnp.arange(half, dtype=F32) / half))[:, None]
    pos = positions.reshape(batch * tiles, 1, ts)
    wt, b_col = w_qkv.T.astype(BF16), b_qkv[:, None]
    w_q, b_q = wt[:C_Q], b_col[:C_Q]
    dup = lambda a: jnp.repeat(a.reshape(C_KV_HEADS, 1, C_HD, -1), 2, 1).reshape(2 * C_KV, -1)
    w_k, b_k = dup(wt[C_Q:C_Q + C_KV]), dup(b_col[C_Q:C_Q + C_KV])
    w_v, b_v = wt[C_Q + C_KV:], b_col[C_Q + C_KV:]
    sink_rows = jnp.repeat(sinks.reshape(C_KV_HEADS, 2, 2).transpose(2, 0, 1), WINDOW, -1)[:, :, None, :]
    const = lambda shape: pl.BlockSpec(shape, lambda b, t: (0,) * len(shape),
                                       pipeline_mode=pl.Buffered(1))
    return pl.pallas_call(
        _mixer_c_kernel,
        grid=(batch, tiles),
        in_specs=[
            pl.BlockSpec((ts, D_MODEL), lambda b, t: (b * tiles + t, 0)),
            pl.BlockSpec((None, 1, ts), lambda b, t: (b * tiles + t, 0, 0)),
            const((half, 1)),
            const((2, C_KV_HEADS, 1, 2 * WINDOW)),
            const((1, D_MODEL)),
            const((C_Q, D_MODEL)),
            const((C_Q, 1)),
            const((2 * C_KV, D_MODEL)),
            const((2 * C_KV, 1)),
            const((C_KV, D_MODEL)),
            const((C_KV, 1)),
        ],
        out_specs=pl.BlockSpec((ts, C_Q), lambda b, t: (b * tiles + t, 0)),
        out_shape=jax.ShapeDtypeStruct((tokens, C_Q), BF16),
        scratch_shapes=[
            pltpu.VMEM((C_Q, ts), BF16),
            pltpu.VMEM((WINDOW + ts, 2 * C_KV), BF16),
            pltpu.VMEM((WINDOW + ts, 2 * C_KV), BF16),
            pltpu.VMEM((2 * C_KV, WINDOW + ts), BF16),
            pltpu.VMEM((2 * C_KV, WINDOW + ts), BF16),
        ],
        compiler_params=pltpu.CompilerParams(
            dimension_semantics=("arbitrary", "arbitrary"),
            vmem_limit_bytes=VMEM_LIMIT_BYTES),
        name="mixer_c",
    )(h, pos, inv_col, sink_rows, g0[None], w_q, b_q, w_k, b_k, w_v, b_v)


def _post_kernel(mix_ref, h_ref, p_ref, wout_ref, bout_ref, gains_ref, wup_ref, wdown_ref,
                 wgate_ref, wple_ref, out_ref):
    mix = jnp.dot(mix_ref[...], wout_ref[...], preferred_element_type=F32) + bout_ref[...]
    h = h_ref[...] + _rms(mix, gains_ref[1:2, :])
    hn = _rms(h, gains_ref[2:3, :]).astype(BF16)
    slab = D_FF // FF_SPLIT
    ff = None
    for j in range(FF_SPLIT):
        up = jnp.dot(hn, wup_ref[:, j * slab:(j + 1) * slab], preferred_element_type=F32)
        act = jnp.square(jnp.maximum(up, 0.0)).astype(BF16)
        part = jnp.dot(act, wdown_ref[j * slab:(j + 1) * slab, :], preferred_element_type=F32)
        ff = part if ff is None else ff + part
    h = h + _rms(ff, gains_ref[3:4, :])
    gate = _sigmoid(jnp.dot(h.astype(BF16), wgate_ref[...], preferred_element_type=F32))
    ple = jnp.dot(p_ref[...].astype(BF16), wple_ref[...], preferred_element_type=F32)
    out_ref[...] = h + gate * ple


def _post(mixed, h, p_all, layer, w_out, b_out, gains, w_up, w_down, w_gate, w_ple):
    tokens = h.shape[0]
    tm = min(POST_TILE, tokens)
    const = lambda shape: pl.BlockSpec(shape, lambda i: (0,) * len(shape),
                                       pipeline_mode=pl.Buffered(1))
    return pl.pallas_call(
        _post_kernel,
        grid=(tokens // tm,),
        in_specs=[
            pl.BlockSpec((tm, D_MODEL), lambda i: (i, 0)),
            pl.BlockSpec((tm, D_MODEL), lambda i: (i, 0)),
            pl.BlockSpec((None, tm, PLE_DIM), lambda i: (layer, i, 0)),
            const((D_MODEL, D_MODEL)),
            const((1, D_MODEL)),
            const((4, D_MODEL)),
            const((D_MODEL, D_FF)),
            const((D_FF, D_MODEL)),
            const((D_MODEL, D_MODEL)),
            const((PLE_DIM, D_MODEL)),
        ],
        out_specs=pl.BlockSpec((tm, D_MODEL), lambda i: (i, 0)),
        out_shape=jax.ShapeDtypeStruct((tokens, D_MODEL), F32),
        compiler_params=pltpu.CompilerParams(
            dimension_semantics=("arbitrary",),
            vmem_limit_bytes=VMEM_LIMIT_BYTES),
        name="post",
    )(mixed, h, p_all, w_out.astype(BF16), b_out[None], gains, w_up.astype(BF16),
      w_down.astype(BF16), w_gate.astype(BF16), w_ple.astype(BF16))


def kernel(x, p, positions, norm_gains, w_in_ab, conv_a, a_log, dt_bias, norm_a, i_bias_b, f_bias_b, norm_b, w_out_ab, w_qkv_c, b_qkv_c, sinks_c, w_o_c, b_o_c, w_up, w_down, w_ple, w_ple_gate):
    batch, seq, _ = x.shape
    depth = norm_gains.shape[0]
    tokens = batch * seq
    h = x.reshape(tokens, D_MODEL)
    p_all = p.reshape(depth, tokens, PLE_DIM)
    for layer in range(depth):
        gains = norm_gains[layer]
        if layer % 2 == 0:
            e = layer // 2
            mixed = _mixer_ab(h, gains[0], w_in_ab[e], conv_a[e], a_log[e], dt_bias[e], norm_a[e],
                              i_bias_b[e], f_bias_b[e], norm_b[e], batch, seq)
            w_out, b_out = w_out_ab[e], jnp.zeros((D_MODEL,), F32)
        else:
            o = layer // 2
            mixed = _mixer_c(h, gains[0], positions, w_qkv_c[o], b_qkv_c[o], sinks_c[o], batch, seq)
            w_out, b_out = w_o_c[o], b_o_c[o]
        h = _post(mixed, h, p_all, layer, w_out, b_out, gains, w_up[layer], w_down[layer],
                  w_ple_gate[layer], w_ple[layer])
    return h.reshape(batch, seq, D_MODEL)
```

```python
import jax
import jax.numpy as jnp
from jax import lax
from jax.experimental import pallas as pl
from jax.experimental.pallas import tpu as pltpu

F32 = jnp.float32
BF16 = jnp.bfloat16
HIGHEST = lax.Precision.HIGHEST

EPS = 1e-6
D_MODEL = 1024
HEADS_AB = 4
HEAD_DIM_AB = 128
CONV_K = 4
CHUNK = 64
C_HEADS = 16
C_KV_HEADS = 4
C_HD = 64
WINDOW = 128
ROPE_THETA = 10000.0
D_FF = 4 * D_MODEL
PLE_DIM = 256
QK_AB = HEADS_AB * HEAD_DIM_AB
CONV_CH = 3 * QK_AB
BIG_COLS = 8 * QK_AB
GATE_COLS = 128
C_Q = C_HEADS * C_HD
C_KV = C_KV_HEADS * C_HD

AB_ROWS = 2
AB_TILE = 256
C_TILE = 2048
POST_TILE = 1024
PREP_CHUNKS = 8
SCORES_AHEAD = 8
SEQ_UNROLL = 4
FF_SPLIT = 8
VMEM_LIMIT_BYTES = 56 * 1024 * 1024


def _rms(x, g):
    return x * lax.rsqrt(jnp.mean(x * x, -1, keepdims=True) + EPS) * g


def _dot(a, b):
    return jnp.dot(a.astype(BF16), b.astype(BF16), preferred_element_type=F32)


def _dot_nt(a, b):
    return lax.dot_general(a.astype(BF16), b.astype(BF16), (((1,), (1,)), ((), ())),
                           preferred_element_type=F32)


def _dot_hi(a, b):
    return jnp.dot(a, b, preferred_element_type=F32, precision=HIGHEST)


def _split(a):
    hi = a.astype(BF16)
    hi_f = hi.astype(F32)
    lo_f = a - hi_f
    left = lax.broadcasted_iota(jnp.int32, a.shape, 2) < a.shape[2] // 2
    return hi, lo_f.astype(BF16), jnp.where(left, hi_f, lo_f).astype(BF16)


def _dot3(a, b):
    (ah, _, ahl), (bh, bl, _) = a, b
    lhs = jnp.concatenate([ahl, ah], 2)
    rhs = jnp.concatenate([bl, bh, bh, jnp.zeros_like(bh)], 1)
    return lax.dot_general(lhs, rhs, (((2,), (1,)), ((0,), (0,))), preferred_element_type=F32)


def _softplus(x):
    return jnp.maximum(x, 0.0) + jnp.log1p(jnp.exp(-jnp.abs(x)))


def _sigmoid(x):
    return 1.0 / (1.0 + jnp.exp(-x))


def _silu(x):
    return x * _sigmoid(x)


def _mixer_ab_kernel(h_ref, g0_ref, wbig_ref, wgate_ref, convw_ref, gprm_ref, norma_ref,
                     normb_ref, out_ref,
                     proj_ref, xbuf_ref, qkv_ref, gate_ref, gcum_ref, gt_ref, gcumt_ref, gcb_ref, bcb_ref,
                     u_ref, wq_ref, attn_ref, kdect_ref, eg_ref,
                     qbb_ref, pv0_ref, rs0_ref, mib_ref, kn0_ref, bl_ref, wm_ref,
                     sa_ref, cn_ref, mb_ref):
    n_rows, tile = h_ref.shape[0], h_ref.shape[1]
    ts = n_rows * tile
    n_chunks = ts // CHUNK
    chunks_per_row = tile // CHUNK
    t = pl.program_id(1)

    @pl.when(t == 0)
    def _():
        xbuf_ref[:, 0:8, :] = jnp.zeros((n_rows, 8, CONV_CH), F32)
        sa_ref[...] = jnp.zeros_like(sa_ref)
        cn_ref[...] = jnp.zeros_like(cn_ref)
        mb_ref[...] = jnp.zeros_like(mb_ref)

    hn = _rms(h_ref[...].reshape(ts, D_MODEL), g0_ref[...]).astype(BF16)
    gates_pre = jnp.dot(hn, wgate_ref[...], preferred_element_type=F32) + gprm_ref[0:1, :]

    lane = lax.broadcasted_iota(jnp.int32, (1, GATE_COLS), 1)
    neg_a = -jnp.exp(gprm_ref[1:2, :])
    gate = jnp.where(lane < 4, _sigmoid(gates_pre),
                     jnp.where(lane < 8, neg_a * _softplus(gates_pre),
                               jnp.where(lane < 12, gates_pre, -_softplus(-gates_pre))))
    gate_ref[...] = gate
    row = lax.broadcasted_iota(jnp.int32, (CHUNK, CHUNK), 0)
    col = lax.broadcasted_iota(jnp.int32, (CHUNK, CHUNK), 1)
    causal = row >= col
    tri = causal.astype(F32)
    row2 = lax.broadcasted_iota(jnp.int32, (CHUNK, 2 * CHUNK), 0)
    col2 = lax.broadcasted_iota(jnp.int32, (CHUNK, 2 * CHUNK), 1) % CHUNK
    causal2 = row2 >= col2
    strict2 = row2 > col2
    eye2 = (row2 == col2).astype(F32)
    twice = lambda x: jnp.concatenate([x, x], 1)
    for c in range(n_chunks):
        gch = gate_ref[c * CHUNK:(c + 1) * CHUNK, :]
        cum = _dot_hi(tri, gch)
        gcum_ref[c * CHUNK:(c + 1) * CHUNK, :] = cum
        gt_ref[c] = twice(gch.T[0:16, :])
        gcumt_ref[c] = twice(cum.T[0:16, :])
    for hd in range(HEADS_AB):
        gcb_ref[hd] = jnp.broadcast_to(gcum_ref[:, 4 + hd:5 + hd], (ts, HEAD_DIM_AB))
        bcb_ref[hd] = jnp.broadcast_to(gcum_ref[:, 12 + hd:13 + hd], (ts, HEAD_DIM_AB))

    proj_ref[...] = jnp.dot(hn, wbig_ref[...], preferred_element_type=F32)
    proj_ref[:, 3 * QK_AB:4 * QK_AB] = _silu(proj_ref[:, 3 * QK_AB:4 * QK_AB])
    proj_ref[:, 7 * QK_AB:8 * QK_AB] = _sigmoid(proj_ref[:, 7 * QK_AB:8 * QK_AB])

    for r in range(n_rows):
        xbuf_ref[r, 8:8 + tile, :] = proj_ref[r * tile:(r + 1) * tile, 0:CONV_CH]
        xfull = xbuf_ref[r]
        conv = convw_ref[0:1, :] * xfull
        for k in range(1, CONV_K):
            conv = convw_ref[k:k + 1, :] * xfull + pltpu.roll(conv, 1, 0)
        xbuf_ref[r, 0:8, :] = xbuf_ref[r, tile:tile + 8, :]
        qkv_ref[r * tile:(r + 1) * tile, :] = _silu(conv[8:, :])
    for hd in range(HEADS_AB):
        sl = slice(hd * HEAD_DIM_AB, (hd + 1) * HEAD_DIM_AB)
        q = qkv_ref[:, sl]
        qkv_ref[:, sl] = q * lax.rsqrt(jnp.sum(q * q, -1, keepdims=True) + EPS) * (HEAD_DIM_AB ** -0.5)
        sl = slice(QK_AB + hd * HEAD_DIM_AB, QK_AB + (hd + 1) * HEAD_DIM_AB)
        k = qkv_ref[:, sl]
        qkv_ref[:, sl] = k * lax.rsqrt(jnp.sum(k * k, -1, keepdims=True) + EPS)

    norm_a = norma_ref[...]

    def prep_body(i, carry):
        lmats, rhs = [], []
        for j in range(PREP_CHUNKS):
            c = i * PREP_CHUNKS + j
            r0 = pl.multiple_of(c * CHUNK, CHUNK)
            rows = pl.ds(r0, CHUNK)
            gct = gcumt_ref[c]
            for hd in range(HEADS_AB):
                lo, hi = hd * HEAD_DIM_AB, (hd + 1) * HEAD_DIM_AB
                q = qkv_ref[rows, lo:hi]
                k = qkv_ref[rows, QK_AB + lo:QK_AB + hi]
                v = qkv_ref[rows, 2 * QK_AB + lo:2 * QK_AB + hi]
                gtc = gt_ref[c]
                beta_row = gtc[hd:hd + 1, :]
                gc_row = gct[4 + hd:5 + hd, :]
                gc_all = gcb_ref[hd, rows, :]
                g_last = gc_all[CHUNK - 1:CHUNK, :]
                diff = gc_all - gc_row
                decay = jnp.where(causal2, jnp.exp(jnp.where(causal2, diff, 0.0)), 0.0)
                egc = jnp.exp(gc_all)
                k_bf = k.astype(BF16)
                kk = _dot_nt(k_bf, jnp.concatenate([k_bf, k_bf], 0))
                lmats.append(jnp.where(strict2, kk * decay, 0.0) * beta_row)
                rhs.append((c, hd, v.astype(BF16), (k * egc).astype(BF16)))
                wq_ref[c, hd, CHUNK:2 * CHUNK, :] = (q * egc).astype(BF16)
                attn_ref[c, hd] = (_dot_nt(q, k_bf) * (decay * beta_row)[:, 0:CHUNK]).astype(BF16)
                kdect_ref[c, hd] = ((k * jnp.exp(g_last - gc_all)).T * beta_row[:, 0:CHUNK]).astype(BF16)
                eg_ref[c, hd] = jnp.exp(g_last)
                qb = proj_ref[rows, 4 * QK_AB + lo:4 * QK_AB + hi] * (HEAD_DIM_AB ** -0.5)
                kbb = proj_ref[rows, 5 * QK_AB + lo:5 * QK_AB + hi]
                vbb = proj_ref[rows, 6 * QK_AB + lo:6 * QK_AB + hi].astype(BF16)
                bc_all = bcb_ref[hd, rows, :]
                bc_row = gct[12 + hd:13 + hd, 0:CHUNK]
                i_row = gtc[8 + hd:9 + hd, 0:CHUNK]
                b_last = bc_all[CHUNK - 1:CHUNK, :]
                dmat = jnp.where(causal, bc_all[:, 0:CHUNK] - bc_row + i_row, -jnp.inf)
                m_intra = jnp.max(dmat, -1, keepdims=True)
                p0 = _dot_nt(qb, kbb) * jnp.exp(dmat - m_intra)
                qbb_ref[c, hd] = qb.astype(BF16)
                pv0_ref[c, hd] = _dot(p0, vbb)
                rs0_ref[c, hd] = jnp.broadcast_to(jnp.sum(p0, -1, keepdims=True), (CHUNK, HEAD_DIM_AB))
                mib_ref[c, hd] = jnp.broadcast_to(m_intra, (CHUNK, HEAD_DIM_AB))
                ws_row = b_last[:, 0:CHUNK] - bc_row + i_row
                ws_max = jnp.max(ws_row, -1, keepdims=True)
                kw_t = kbb.T * jnp.exp(ws_row - ws_max)
                kn0_ref[c, hd, :, 0:HEAD_DIM_AB] = _dot(kw_t, vbb)
                kn0_ref[c, hd, :, HEAD_DIM_AB:] = jnp.broadcast_to(
                    jnp.sum(kw_t, -1, keepdims=True), (HEAD_DIM_AB, HEAD_DIM_AB))
                bl_ref[c, hd] = jnp.concatenate([b_last, b_last], 1)
                wm_ref[c, hd] = jnp.broadcast_to(ws_max, (1, 2 * HEAD_DIM_AB))
        lmat = jnp.stack(lmats)
        l_split = _split(lmat)
        tinv = eye2 - lmat
        pw = _dot3(l_split, l_split)
        for _ in range(4):
            pw_split = _split(pw)
            tinv = tinv + _dot3(_split(tinv), pw_split)
            pw = _dot3(pw_split, pw_split)
        tinv = (tinv + _dot3(_split(tinv), _split(pw)))[:, :, 0:CHUNK].astype(BF16)
        for n, (c, hd, v_bf, ke_bf) in enumerate(rhs):
            u_ref[c, hd] = jnp.dot(tinv[n], v_bf, preferred_element_type=F32)
            wq_ref[c, hd, 0:CHUNK, :] = jnp.dot(tinv[n], ke_bf, preferred_element_type=F32).astype(BF16)
        return carry

    lax.fori_loop(0, n_chunks // PREP_CHUNKS, prep_body, 0)

    def chunk_body(j, carry):
        local = pl.ds(pl.multiple_of(j * CHUNK, CHUNK), CHUNK)
        rows = [pl.ds(pl.multiple_of(r * tile + j * CHUNK, CHUNK), CHUNK) for r in range(n_rows)]
        both = lambda ref: jnp.concatenate([ref[r * chunks_per_row + j] for r in range(n_rows)], 0)
        bdot = lambda x, y: lax.dot_general(x, y, (((2,), (1,)), ((0,), (0,))),
                                            preferred_element_type=F32)
        state = sa_ref[...]
        ws_qs = bdot(both(wq_ref), state.astype(BF16))
        cn = cn_ref[...]
        q_cn = bdot(both(qbb_ref), cn.astype(BF16))

        v_new = (both(u_ref) - ws_qs[:, 0:CHUNK]).astype(BF16)
        o = ws_qs[:, CHUNK:] + bdot(both(attn_ref), v_new)
        sa_ref[...] = state * both(eg_ref) + bdot(both(kdect_ref), v_new)

        m_prev = mb_ref[:, 0:1, :]
        a = jnp.concatenate([bcb_ref[:, rw, :] for rw in rows], 0) + m_prev[:, :, 0:HEAD_DIM_AB]
        m_intra = both(mib_ref)
        mt = jnp.maximum(a, m_intra)
        inter = jnp.exp(a - mt)
        intra = jnp.exp(m_intra - mt)
        num = inter * q_cn[:, :, 0:HEAD_DIM_AB] + intra * both(pv0_ref)
        den = inter * q_cn[:, :, HEAD_DIM_AB:] + intra * both(rs0_ref)
        hout = num / jnp.maximum(jnp.abs(den), jnp.exp(-mt))
        b_last, ws_max = both(bl_ref), both(wm_ref)
        m_new = jnp.maximum(b_last + m_prev, ws_max)
        cn_ref[...] = jnp.exp(b_last + m_prev - m_new) * cn + jnp.exp(ws_max - m_new) * both(kn0_ref)
        mb_ref[...] = jnp.broadcast_to(m_new, mb_ref.shape)

        for r in range(n_rows):
            for hd in range(HEADS_AB):
                n = r * HEADS_AB + hd
                lo, hi = hd * HEAD_DIM_AB, (hd + 1) * HEAD_DIM_AB
                z_gate = proj_ref[rows[r], 3 * QK_AB + lo:3 * QK_AB + hi]
                out_ref[r, local, lo:hi] = (_rms(o[n], norm_a) * z_gate).astype(out_ref.dtype)
                o_gate = proj_ref[rows[r], 7 * QK_AB + lo:7 * QK_AB + hi]
                out_ref[r, local, QK_AB + lo:QK_AB + hi] = (
                    _rms(hout[n], normb_ref[hd:hd + 1, :]) * o_gate).astype(out_ref.dtype)
        return carry

    lax.fori_loop(0, chunks_per_row, chunk_body, 0, unroll=SEQ_UNROLL)


def _mixer_ab(h, g0, w_in, conv_w, a_log, dt_bias, norm_a, i_bias, f_bias, norm_b, batch, seq):
    tile = min(AB_TILE, seq)
    assert batch % AB_ROWS == 0 and seq % tile == 0, (batch, seq)
    n_rows = AB_ROWS
    tiles = seq // tile
    ts = n_rows * tile
    a_cols = 2 * QK_AB + QK_AB + 2 * HEADS_AB + QK_AB
    wa, wb = w_in[:, :a_cols], w_in[:, a_cols:]
    g_off = 3 * QK_AB
    z_off = g_off + 2 * HEADS_AB
    w_big = jnp.concatenate([wa[:, :g_off], wa[:, z_off:], wb[:, :g_off], wb[:, z_off:]], 1).astype(BF16)
    w_gate = jnp.concatenate([wa[:, g_off:z_off], wb[:, g_off:z_off],
                              jnp.zeros((D_MODEL, GATE_COLS - 4 * HEADS_AB), F32)], 1).astype(BF16)
    zeros4 = jnp.zeros((HEADS_AB,), F32)
    pad = jnp.zeros((GATE_COLS - 4 * HEADS_AB,), F32)
    bias_row = jnp.concatenate([zeros4, dt_bias, i_bias, f_bias, pad])
    alog_row = jnp.concatenate([zeros4, a_log, zeros4, zeros4, pad])
    gprm = jnp.concatenate([bias_row[None], alog_row[None], jnp.zeros((6, GATE_COLS), F32)], 0)

    const = lambda shape: pl.BlockSpec(shape, lambda b, t: (0,) * len(shape),
                                       pipeline_mode=pl.Buffered(1))
    n_chunks = ts // CHUNK
    per_head = (n_chunks, HEADS_AB)
    all_heads = n_rows * HEADS_AB
    return pl.pallas_call(
        _mixer_ab_kernel,
        grid=(batch // n_rows, tiles),
        in_specs=[
            pl.BlockSpec((n_rows, tile, D_MODEL), lambda b, t: (b, t, 0)),
            const((1, D_MODEL)),
            const((D_MODEL, BIG_COLS)),
            const((D_MODEL, GATE_COLS)),
            const((CONV_K, CONV_CH)),
            const((8, GATE_COLS)),
            const((1, HEAD_DIM_AB)),
            const((HEADS_AB, HEAD_DIM_AB)),
        ],
        out_specs=pl.BlockSpec((n_rows, tile, 2 * QK_AB), lambda b, t: (b, t, 0)),
        out_shape=jax.ShapeDtypeStruct((batch, seq, 2 * QK_AB), BF16),
        scratch_shapes=[
            pltpu.VMEM((ts, BIG_COLS), F32),
            pltpu.VMEM((n_rows, tile + 8, CONV_CH), F32),
            pltpu.VMEM((ts, CONV_CH), F32),
            pltpu.VMEM((ts, GATE_COLS), F32),
            pltpu.VMEM((ts, GATE_COLS), F32),
            pltpu.VMEM((n_chunks, 16, 2 * CHUNK), F32),
            pltpu.VMEM((n_chunks, 16, 2 * CHUNK), F32),
            pltpu.VMEM((HEADS_AB, ts, HEAD_DIM_AB), F32),
            pltpu.VMEM((HEADS_AB, ts, HEAD_DIM_AB), F32),
            pltpu.VMEM(per_head + (CHUNK, HEAD_DIM_AB), F32),
            pltpu.VMEM(per_head + (2 * CHUNK, HEAD_DIM_AB), BF16),
            pltpu.VMEM(per_head + (CHUNK, CHUNK), BF16),
            pltpu.VMEM(per_head + (HEAD_DIM_AB, CHUNK), BF16),
            pltpu.VMEM(per_head + (1, HEAD_DIM_AB), F32),
            pltpu.VMEM(per_head + (CHUNK, HEAD_DIM_AB), BF16),
            pltpu.VMEM(per_head + (CHUNK, HEAD_DIM_AB), F32),
            pltpu.VMEM(per_head + (CHUNK, HEAD_DIM_AB), F32),
            pltpu.VMEM(per_head + (CHUNK, HEAD_DIM_AB), F32),
            pltpu.VMEM(per_head + (HEAD_DIM_AB, 2 * HEAD_DIM_AB), F32),
            pltpu.VMEM(per_head + (1, 2 * HEAD_DIM_AB), F32),
            pltpu.VMEM(per_head + (1, 2 * HEAD_DIM_AB), F32),
            pltpu.VMEM((all_heads, HEAD_DIM_AB, HEAD_DIM_AB), F32),
            pltpu.VMEM((all_heads, HEAD_DIM_AB, 2 * HEAD_DIM_AB), F32),
            pltpu.VMEM((all_heads, 8, 2 * HEAD_DIM_AB), F32),
        ],
        compiler_params=pltpu.CompilerParams(
            dimension_semantics=("arbitrary", "arbitrary"),
            vmem_limit_bytes=VMEM_LIMIT_BYTES),
        name="mixer_ab",
    )(h.reshape(batch, seq, D_MODEL), g0[None], w_big, w_gate, conv_w, gprm, norm_a[None],
      norm_b).reshape(batch * seq, 2 * QK_AB)


def _mixer_c_kernel(h_ref, pos_ref, invc_ref, sink_ref, g0_ref, wqt_ref, bq_ref, wkt_ref, bk_ref,
                    wvt_ref, bv_ref, out_ref, q_ref, klo_ref, khi_ref, vtlo_ref, vthi_ref):
    ts = h_ref.shape[0]
    n_blocks = ts // WINDOW
    half = C_HD // 2
    t = pl.program_id(1)

    @pl.when(t == 0)
    def _():
        vtlo_ref[...] = jnp.zeros_like(vtlo_ref)
        vthi_ref[...] = jnp.zeros_like(vthi_ref)
        klo_ref[0:WINDOW, :] = jnp.zeros((WINDOW, klo_ref.shape[1]), BF16)
        khi_ref[0:WINDOW, :] = jnp.zeros((WINDOW, khi_ref.shape[1]), BF16)

    hn = _rms(h_ref[...], g0_ref[...]).astype(BF16)

    ang_t = pos_ref[...].astype(F32) * invc_ref[...]
    cos_t = jnp.cos(ang_t)
    sin_t = jnp.sin(ang_t)

    def rope_t(x):
        x1, x2 = x[0:half], x[half:C_HD]
        return jnp.concatenate([x1 * cos_t - x2 * sin_t, x2 * cos_t + x1 * sin_t], 0)

    qt = _dot_nt(wqt_ref[...], hn) + bq_ref[...]
    for hd in range(C_HEADS):
        rows = slice(hd * C_HD, (hd + 1) * C_HD)
        q_ref[rows, :] = (rope_t(qt[rows]) * (C_HD ** -0.5)).astype(BF16)

    kt = _dot_nt(wkt_ref[...], hn) + bk_ref[...]
    kk = jnp.concatenate([rope_t(kt[j * C_HD:(j + 1) * C_HD]) for j in range(C_KV_HEADS)], 0).T
    low = lax.broadcasted_iota(jnp.int32, (1, 128), 1) < C_HD
    for j in range(C_KV // 128):
        pair = kk[:, j * 128:(j + 1) * 128]
        swapped = pltpu.roll(pair, C_HD, 1)
        base = 2 * j * 128
        klo_ref[WINDOW:, base:base + 128] = jnp.where(low, pair, 0.0).astype(BF16)
        khi_ref[WINDOW:, base:base + 128] = jnp.where(low, 0.0, swapped).astype(BF16)
        klo_ref[WINDOW:, base + 128:base + 256] = jnp.where(low, swapped, 0.0).astype(BF16)
        khi_ref[WINDOW:, base + 128:base + 256] = jnp.where(low, 0.0, pair).astype(BF16)

    vt = (_dot_nt(wvt_ref[...], hn) + bv_ref[...]).astype(BF16)
    for g in range(C_KV_HEADS):
        vtlo_ref[2 * g * C_HD:(2 * g + 1) * C_HD, WINDOW:] = vt[g * C_HD:(g + 1) * C_HD]
        vthi_ref[(2 * g + 1) * C_HD:(2 * g + 2) * C_HD, WINDOW:] = vt[g * C_HD:(g + 1) * C_HD]

    bdot = lambda x, y: lax.dot_general(x, y, (((2,), (1,)), ((0,), (0,))), preferred_element_type=F32)
    groups = range(C_KV_HEADS)
    key_idx = lax.broadcasted_iota(jnp.int32, (WINDOW, 2 * WINDOW), 0)
    qry_idx = lax.broadcasted_iota(jnp.int32, (WINDOW, 2 * WINDOW), 1) % WINDOW
    from_prev = key_idx > qry_idx
    def scores(blk):
        cols = slice(blk * WINDOW, (blk + 1) * WINDOW)
        win = slice(blk * WINDOW, (blk + 2) * WINDOW)
        kbd = jnp.stack([jnp.concatenate([klo_ref[win, g * 128:(g + 1) * 128],
                                          khi_ref[win, g * 128:(g + 1) * 128]], 0) for g in groups])
        qbd = jnp.stack([jnp.concatenate([q_ref[g * 256:g * 256 + 128, cols],
                                          q_ref[g * 256 + 128:(g + 1) * 256, cols]], 1) for g in groups])
        return bdot(kbd, qbd)

    pending = [scores(blk) for blk in range(min(SCORES_AHEAD, n_blocks))]
    for blk in range(n_blocks):
        cols = slice(blk * WINDOW, (blk + 1) * WINDOW)
        win = slice(blk * WINDOW, (blk + 2) * WINDOW)
        st = pending.pop(0)
        if blk + SCORES_AHEAD < n_blocks:
            pending.append(scores(blk + SCORES_AHEAD))
        probs = []
        for odd in range(2):
            s_prev = st[:, odd * 2 * WINDOW:odd * 2 * WINDOW + WINDOW]
            s_cur = st[:, odd * 2 * WINDOW + WINDOW:(odd + 1) * 2 * WINDOW]
            sc = jnp.where(from_prev, s_prev, s_cur)
            if blk == 0:
                sc = jnp.where(from_prev & (key_idx < jnp.where(t > 0, -1, WINDOW)), -jnp.inf, sc)
            sink = sink_ref[odd]
            m = jnp.maximum(jnp.max(sc, 1, keepdims=True), sink)
            pr = jnp.exp(sc - m)
            den = jnp.sum(pr, 1, keepdims=True) + jnp.exp(sink - m)
            pr = pr / den
            probs += [jnp.where(from_prev, pr, 0.0).astype(BF16), jnp.where(from_prev, 0.0, pr).astype(BF16)]
        vbd = jnp.stack([jnp.concatenate([vtlo_ref[g * 128:(g + 1) * 128, win],
                                          vthi_ref[g * 128:(g + 1) * 128, win]], 1) for g in groups])
        ot = bdot(vbd, jnp.concatenate(probs, 1))
        for g in groups:
            o = ot[g].T
            out_ref[cols, g * 256:g * 256 + 128] = o[0:WINDOW].astype(out_ref.dtype)
            out_ref[cols, g * 256 + 128:(g + 1) * 256] = o[WINDOW:].astype(out_ref.dtype)

    klo_ref[0:WINDOW, :] = klo_ref[ts:ts + WINDOW, :]
    khi_ref[0:WINDOW, :] = khi_ref[ts:ts + WINDOW, :]
    vtlo_ref[:, 0:WINDOW] = vtlo_ref[:, ts:ts + WINDOW]
    vthi_ref[:, 0:WINDOW] = vthi_ref[:, ts:ts + WINDOW]


def _mixer_c(h, g0, positions, w_qkv, b_qkv, sinks, batch, seq):
    tokens = batch * seq
    ts = min(C_TILE, seq)
    tiles = seq // ts
    half = C_HD // 2
    inv_col = (ROPE_THETA ** (-jnp.arange(half, dtype=F32) / half))[:, None]
    pos = positions.reshape(batch * tiles, 1, ts)
    wt, b_col = w_qkv.T.astype(BF16), b_qkv[:, None]
    w_q, b_q = wt[:C_Q], b_col[:C_Q]
    w_k, b_k = wt[C_Q:C_Q + C_KV], b_col[C_Q:C_Q + C_KV]
    w_v, b_v = wt[C_Q + C_KV:], b_col[C_Q + C_KV:]
    sink_rows = jnp.repeat(sinks.reshape(C_KV_HEADS, 2, 2).transpose(2, 0, 1), WINDOW, -1)[:, :, None, :]
    const = lambda shape: pl.BlockSpec(shape, lambda b, t: (0,) * len(shape),
                                       pipeline_mode=pl.Buffered(1))
    return pl.pallas_call(
        _mixer_c_kernel,
        grid=(batch, tiles),
        in_specs=[
            pl.BlockSpec((ts, D_MODEL), lambda b, t: (b * tiles + t, 0)),
            pl.BlockSpec((None, 1, ts), lambda b, t: (b * tiles + t, 0, 0)),
            const((half, 1)),
            const((2, C_KV_HEADS, 1, 2 * WINDOW)),
            const((1, D_MODEL)),
            const((C_Q, D_MODEL)),
            const((C_Q, 1)),
            const((C_KV, D_MODEL)),
            const((C_KV, 1)),
            const((C_KV, D_MODEL)),
            const((C_KV, 1)),
        ],
        out_specs=pl.BlockSpec((ts, C_Q), lambda b, t: (b * tiles + t, 0)),
        out_shape=jax.ShapeDtypeStruct((tokens, C_Q), BF16),
        scratch_shapes=[
            pltpu.VMEM((C_Q, ts), BF16),
            pltpu.VMEM((WINDOW + ts, 2 * C_KV), BF16),
            pltpu.VMEM((WINDOW + ts, 2 * C_KV), BF16),
            pltpu.VMEM((2 * C_KV, WINDOW + ts), BF16),
            pltpu.VMEM((2 * C_KV, WINDOW + ts), BF16),
        ],
        compiler_params=pltpu.CompilerParams(
            dimension_semantics=("arbitrary", "arbitrary"),
            vmem_limit_bytes=VMEM_LIMIT_BYTES),
        name="mixer_c",
    )(h, pos, inv_col, sink_rows, g0[None], w_q, b_q, w_k, b_k, w_v, b_v)


def _post_kernel(mix_ref, h_ref, p_ref, wout_ref, bout_ref, gains_ref, wup_ref, wdown_ref,
                 wgate_ref, wple_ref, out_ref):
    mix = jnp.dot(mix_ref[...], wout_ref[...], preferred_element_type=F32) + bout_ref[...]
    h = h_ref[...] + _rms(mix, gains_ref[1:2, :])
    hn = _rms(h, gains_ref[2:3, :]).astype(BF16)
    slab = D_FF // FF_SPLIT
    ff = None
    for j in range(FF_SPLIT):
        up = jnp.dot(hn, wup_ref[:, j * slab:(j + 1) * slab], preferred_element_type=F32)
        act = jnp.square(jnp.maximum(up, 0.0)).astype(BF16)
        part = jnp.dot(act, wdown_ref[j * slab:(j + 1) * slab, :], preferred_element_type=F32)
        ff = part if ff is None else ff + part
    h = h + _rms(ff, gains_ref[3:4, :])
    gate = _sigmoid(jnp.dot(h.astype(BF16), wgate_ref[...], preferred_element_type=F32))
    ple = jnp.dot(p_ref[...].astype(BF16), wple_ref[...], preferred_element_type=F32)
    out_ref[...] = h + gate * ple


def _post(mixed, h, p_all, layer, w_out, b_out, gains, w_up, w_down, w_gate, w_ple):
    tokens = h.shape[0]
    tm = min(POST_TILE, tokens)
    const = lambda shape: pl.BlockSpec(shape, lambda i: (0,) * len(shape),
                                       pipeline_mode=pl.Buffered(1))
    return pl.pallas_call(
        _post_kernel,
        grid=(tokens // tm,),
        in_specs=[
            pl.BlockSpec((tm, D_MODEL), lambda i: (i, 0)),
            pl.BlockSpec((tm, D_MODEL), lambda i: (i, 0)),
            pl.BlockSpec((None, tm, PLE_DIM), lambda i: (layer, i, 0)),
            const((D_MODEL, D_MODEL)),
            const((1, D_MODEL)),
            const((4, D_MODEL)),
            const((D_MODEL, D_FF)),
            const((D_FF, D_MODEL)),
            const((D_MODEL, D_MODEL)),
            const((PLE_DIM, D_MODEL)),
        ],
        out_specs=pl.BlockSpec((tm, D_MODEL), lambda i: (i, 0)),
        out_shape=jax.ShapeDtypeStruct((tokens, D_MODEL), F32),
        compiler_params=pltpu.CompilerParams(
            dimension_semantics=("arbitrary",),
            vmem_limit_bytes=VMEM_LIMIT_BYTES),
        name="post",
    )(mixed, h, p_all, w_out.astype(BF16), b_out[None], gains, w_up.astype(BF16),
      w_down.astype(BF16), w_gate.astype(BF16), w_ple.astype(BF16))


def kernel(x, p, positions, norm_gains, w_in_ab, conv_a, a_log, dt_bias, norm_a, i_bias_b, f_bias_b, norm_b, w_out_ab, w_qkv_c, b_qkv_c, sinks_c, w_o_c, b_o_c, w_up, w_down, w_ple, w_ple_gate):
    batch, seq, _ = x.shape
    depth = norm_gains.shape[0]
    tokens = batch * seq
    h = x.reshape(tokens, D_MODEL)
    p_all = p.reshape(depth, tokens, PLE_DIM)
    for layer in range(depth):
        gains = norm_gains[layer]
        if layer % 2 == 0:
            e = layer // 2
            mixed = _mixer_ab(h, gains[0], w_in_ab[e], conv_a[e], a_log[e], dt_bias[e], norm_a[e],
                              i_bias_b[e], f_bias_b[e], norm_b[e], batch, seq)
            w_out, b_out = w_out_ab[e], jnp.zeros((D_MODEL,), F32)
        else:
            o = layer // 2
            mixed = _mixer_c(h, gains[0], positions, w_qkv_c[o], b_qkv_c[o], sinks_c[o], batch, seq)
            w_out, b_out = w_o_c[o], b_o_c[o]
        h = _post(mixed, h, p_all, layer, w_out, b_out, gains, w_up[layer], w_down[layer],
                  w_ple_gate[layer], w_ple[layer])
    return h.reshape(batch, seq, D_MODEL)
```
